```python
import math
import functools
import jax
import jax.numpy as jnp
from jax import lax
import numpy as np

D_MODEL = 1024
BATCH = 16
SEQ = 2048
DEPTH = 2
DEC_BATCH = 128
DEC_SEQ = 1
PAST_LEN = 8192
PAGE_SIZE = 128

N_META = 16
N_A_LAYERS = DEPTH // 2
N_B_LAYERS = DEPTH - N_A_LAYERS
NORM_EPS = 1e-6

GDN_HEADS = 8
GDN_DK = 128
GDN_DV = 128
GDN_CONV = 4
GDN_CHUNK = 64
GDN_KEY = GDN_HEADS * GDN_DK
GDN_VAL = GDN_HEADS * GDN_DV
GDN_QKV = 2 * GDN_KEY + GDN_VAL
GDN_IN = GDN_QKV + GDN_VAL + 2 * GDN_HEADS

MLA_HEADS = 8
MLA_Q_LORA = 384
MLA_KV_LORA = 256
MLA_NOPE = 128
MLA_ROPE = 64
MLA_V = 128
MLA_SCALE = 1.0 / math.sqrt(MLA_NOPE + MLA_ROPE)
ROPE_THETA = 10000.0
Q_BLOCK = 128

D_FF = 2816
FFN_CONV = 3

kernel_name = 'yoco_gdn_mla_convffn_step'


def rmsnorm(x, w):
    xf = x.astype(jnp.float32)
    y = xf * lax.rsqrt(jnp.mean(xf * xf, axis=-1, keepdims=True) + NORM_EPS)
    return (y * w.astype(jnp.float32)).astype(x.dtype)


def l2norm(x):
    xf = x.astype(jnp.float32)
    return xf * lax.rsqrt(jnp.sum(xf * xf, axis=-1, keepdims=True) + NORM_EPS)


def rope(x, positions):
    half = x.shape[-1] // 2
    inv = ROPE_THETA ** (-jnp.arange(half, dtype=jnp.float32) / half)
    ang = positions.astype(jnp.float32)[:, None] * inv[None, :]
    cos = jnp.cos(ang)[None, :, None, :]
    sin = jnp.sin(ang)[None, :, None, :]
    x1 = x[..., :half].astype(jnp.float32)
    x2 = x[..., half:].astype(jnp.float32)
    return jnp.concatenate([x1 * cos - x2 * sin, x2 * cos + x1 * sin], axis=-1).astype(x.dtype)


def causal_dwconv(x, prev, w):
    width = w.shape[0]
    L = x.shape[1]
    xp = jnp.concatenate([prev.astype(x.dtype), x], axis=1)
    out = sum(xp[:, j:j + L] * w[j] for j in range(width))
    return out, xp[:, L:]


def gated_delta_chunked(q, k, v, g, beta, S0, chunk):
    f32 = jnp.float32
    B, L, H, DK = q.shape
    DV = v.shape[-1]
    n = L // chunk

    def blocks(t):
        t = t.astype(f32).reshape((B, n, chunk, H) + t.shape[3:])
        return jnp.moveaxis(t, (1, 3), (0, 2))

    qc = blocks(q) * (DK ** -0.5)
    kc = blocks(k)
    vc = blocks(v)
    bc = blocks(beta)
    gc = jnp.cumsum(blocks(g), axis=-1)
    idx = jnp.arange(chunk)
    causal = idx[:, None] >= idx[None, :]
    decay = jnp.exp(jnp.where(causal, gc[..., :, None] - gc[..., None, :], -jnp.inf))
    kb = kc * bc[..., None]
    strict = jnp.where(idx[:, None] > idx[None, :],
                       jnp.einsum('nbhcd,nbhed->nbhce', kb, kc) * decay, 0.0)
    rhs = jnp.concatenate([vc * bc[..., None], kb * jnp.exp(gc)[..., None]], axis=-1)
    uw = lax.linalg.triangular_solve(strict, rhs, left_side=True, lower=True, unit_diagonal=True)
    u, w = uw[..., :DV], uw[..., DV:]
    attn = jnp.einsum('nbhcd,nbhed->nbhce', qc, kc) * decay
    g_last = gc[..., -1]
    q_dec = qc * jnp.exp(gc)[..., None]
    k_tail = kc * jnp.exp(g_last[..., None] - gc)[..., None]

    def step(S, xs):
        attn_n, u_n, w_n, q_n, k_n, gl_n = xs
        v_new = u_n - jnp.einsum('bhcd,bhdv->bhcv', w_n, S)
        o_n = jnp.einsum('bhcd,bhdv->bhcv', q_n, S) + jnp.einsum('bhce,bhev->bhcv', attn_n, v_new)
        S = S * jnp.exp(gl_n)[..., None, None] + jnp.einsum('bhcd,bhcv->bhdv', k_n, v_new)
        return S, o_n

    S, o = lax.scan(step, S0.astype(f32), (attn, u, w, q_dec, k_tail, g_last))
    return jnp.moveaxis(o, (0, 2), (1, 3)).reshape(B, L, H, DV), S


def gated_delta_mixer(h, conv_prev, S0, segments, w_in, conv_w, a_log, dt_bias, out_norm, w_out):
    B, L, _ = h.shape
    proj = h @ w_in
    qkv, z, a, b = jnp.split(proj, [GDN_QKV, GDN_QKV + GDN_VAL, GDN_QKV + GDN_VAL + GDN_HEADS], axis=-1)
    qkv, conv_new = causal_dwconv(qkv, conv_prev, conv_w)
    qkv = jax.nn.silu(qkv)
    q, k, v = jnp.split(qkv, [GDN_KEY, 2 * GDN_KEY], axis=-1)
    q = l2norm(q.reshape(B, L, GDN_HEADS, GDN_DK))
    k = l2norm(k.reshape(B, L, GDN_HEADS, GDN_DK))
    v = v.reshape(B, L, GDN_HEADS, GDN_DV)
    beta = jax.nn.sigmoid(b.astype(jnp.float32))
    g = -jnp.exp(a_log.astype(jnp.float32)) * jax.nn.softplus(a.astype(jnp.float32) + dt_bias.astype(jnp.float32))
    S = S0
    outs = []
    for start, stop, chunk in segments:
        o_seg, S = gated_delta_chunked(q[:, start:stop], k[:, start:stop], v[:, start:stop],
                                       g[:, start:stop], beta[:, start:stop], S, chunk)
        outs.append(o_seg)
    o = jnp.concatenate(outs, axis=1)
    o = rmsnorm(o, out_norm) * jax.nn.silu(z.reshape(B, L, GDN_HEADS, GDN_DV).astype(jnp.float32))
    return o.reshape(B, L, GDN_VAL).astype(h.dtype) @ w_out, conv_new, S.astype(S0.dtype)


def mla_shared_kv(x, positions, kv_norm, kv_w_a, kv_a_norm):
    ckv = rmsnorm(x, kv_norm) @ kv_w_a
    c = rmsnorm(ckv[..., :MLA_KV_LORA], kv_a_norm)
    k_rope = rope(ckv[..., None, MLA_KV_LORA:], positions)[..., 0, :]
    return c, k_rope


def latent_scores(q_lat, q_pe, c, k_rope):
    s = jnp.einsum('bqhr,bkr->bhqk', q_lat, c) + jnp.einsum('bqhd,bkd->bhqk', q_pe, k_rope)
    return s.astype(jnp.float32) * MLA_SCALE


def masked_latent_attention(q_lat, q_pe, q_pos, c, k_rope, k_pos):
    s = latent_scores(q_lat, q_pe, c, k_rope)
    s = jnp.where(k_pos[None, None, None, :] <= q_pos[None, None, :, None], s, -jnp.inf)
    prob = jax.nn.softmax(s, axis=-1).astype(c.dtype)
    return jnp.einsum('bhqk,bkr->bqhr', prob, c)


def prompt_attend(q_lat, q_pe, c, k_rope):
    B, L, H, R = q_lat.shape
    pos = jnp.arange(L, dtype=jnp.int32)
    o_meta = masked_latent_attention(q_lat[:, :N_META], q_pe[:, :N_META], pos[:N_META],
                                     c[:, :N_META], k_rope[:, :N_META], pos[:N_META])
    nb = (L - N_META) // Q_BLOCK

    def blocks(t):
        return jnp.moveaxis(t[:, N_META:].reshape((B, nb, Q_BLOCK) + t.shape[2:]), 1, 0)

    o_real = lax.map(lambda a: masked_latent_attention(a[0], a[1], a[2], c, k_rope, pos),
                     (blocks(q_lat), blocks(q_pe), pos[N_META:].reshape(nb, Q_BLOCK)))
    o_real = jnp.moveaxis(o_real, 0, 1).reshape(B, L - N_META, H, R)
    return jnp.concatenate([o_meta, o_real], axis=1)


def sample_attend(q_lat, q_pe, c, k_rope, c_past, k_past):
    T = q_lat.shape[1]
    P = c_past.shape[1]
    s_past = latent_scores(q_lat, q_pe, c_past, k_past)
    idx = jnp.arange(T)
    s_new = jnp.where(idx[None, None, None, :] <= idx[None, None, :, None],
                      latent_scores(q_lat, q_pe, c, k_rope), -jnp.inf)
    prob = jax.nn.softmax(jnp.concatenate([s_past, s_new], axis=-1), axis=-1).astype(c.dtype)
    return (jnp.einsum('bhqk,bkr->bqhr', prob[..., :P], c_past)
            + jnp.einsum('bhqk,bkr->bqhr', prob[..., P:], c))


def mla_mixer(h, positions, c, k_rope, attend, w_q_a, q_a_norm, w_q_b, w_uk, w_uv, w_out):
    B, L, _ = h.shape
    q = (rmsnorm(h @ w_q_a, q_a_norm) @ w_q_b).reshape(B, L, MLA_HEADS, MLA_NOPE + MLA_ROPE)
    q_pe = rope(q[..., MLA_NOPE:], positions)
    q_lat = jnp.einsum('blhn,rhn->blhr', q[..., :MLA_NOPE], w_uk)
    o_lat = attend(q_lat, q_pe, c, k_rope)
    o = jnp.einsum('blhr,rhv->blhv', o_lat, w_uv).reshape(B, L, MLA_HEADS * MLA_V)
    return o @ w_out


def conv_ffn(h, prev, w_up, conv_w, conv_b, w_down):
    u, new_prev = causal_dwconv(h @ w_up, prev, conv_w)
    gate, val = jnp.split(u + conv_b, 2, axis=-1)
    return (jax.nn.silu(gate) * val) @ w_down, new_prev


def trunk(x, positions, segments, delta_S0, delta_conv0, ffn_conv0, attend, p):
    new_S, new_dconv, new_fconv = [], [], []
    c_kv, k_rope = None, None
    for layer in range(DEPTH):
        if layer < N_A_LAYERS:
            i = layer
            o, dconv, S = gated_delta_mixer(rmsnorm(x, p['a_norm_pre'][i]), delta_conv0[i], delta_S0[i], segments,
                                            p['a_w_in'][i], p['a_conv_w'][i], p['a_log'][i], p['a_dt_bias'][i],
                                            p['a_out_norm'][i], p['a_w_out'][i])
            x = x + rmsnorm(o, p['a_norm_post'][i])
            new_S.append(S)
            new_dconv.append(dconv)
        else:
            j = layer - N_A_LAYERS
            if j == 0:
                c_kv, k_rope = mla_shared_kv(x, positions, p['kv_norm'], p['kv_w_a'], p['kv_a_norm'])
            o = mla_mixer(rmsnorm(x, p['b_norm_pre'][j]), positions, c_kv, k_rope, attend,
                          p['b_w_q_a'][j], p['b_q_a_norm'][j], p['b_w_q_b'][j],
                          p['kv_w_uk'], p['kv_w_uv'], p['b_w_out'][j])
            x = x + rmsnorm(o, p['b_norm_post'][j])
        o, fconv = conv_ffn(rmsnorm(x, p['f_norm_pre'][layer]), ffn_conv0[layer],
                            p['f_w_up'][layer], p['f_conv_w'][layer], p['f_conv_b'][layer], p['f_w_down'][layer])
        x = x + rmsnorm(o, p['f_norm_post'][layer])
        new_fconv.append(fconv)
    return x, jnp.stack(new_S), jnp.stack(new_dconv), jnp.stack(new_fconv), c_kv, k_rope


def setup_inputs(seed: int = 0) -> dict:
    key = jax.random.key(seed)
    keys = iter(jax.random.split(key, 48))
    f32 = jnp.float32
    n_pages = PAST_LEN // PAGE_SIZE
    n_pool = (DEC_BATCH * n_pages * 5) // 4

    def normal(shape, scale=1.0):
        return jax.random.normal(next(keys), shape, f32) * scale

    def gain(shape):
        return 1.0 + normal(shape, 0.02)

    perm = jax.random.permutation(next(keys), n_pool)
    dt = jnp.exp(jax.random.uniform(next(keys), (N_A_LAYERS, GDN_HEADS), f32, math.log(1e-3), math.log(1e-1)))
    a_log = jnp.log(jax.random.uniform(next(keys), (N_A_LAYERS, GDN_HEADS), f32, 1.0, 16.0))
    return {
        'x_prompt': normal((BATCH, SEQ, D_MODEL)),
        'x_sample': normal((DEC_BATCH, DEC_SEQ, D_MODEL)),
        'state_delta_S': normal((N_A_LAYERS, DEC_BATCH, GDN_HEADS, GDN_DK, GDN_DV), 0.1),
        'state_delta_conv': normal((N_A_LAYERS, DEC_BATCH, GDN_CONV - 1, GDN_QKV)),
        'state_ffn_conv': normal((DEPTH, DEC_BATCH, FFN_CONV - 1, 2 * D_FF)),
        'cache_kv_latent': normal((n_pool, PAGE_SIZE, MLA_KV_LORA)),
        'cache_k_rope': normal((n_pool, PAGE_SIZE, MLA_ROPE)),
        'page_table': perm[:DEC_BATCH * n_pages].reshape(DEC_BATCH, n_pages).astype(jnp.int32),
        'meta_tokens': normal((N_META, D_MODEL)),
        'a_norm_pre': gain((N_A_LAYERS, D_MODEL)),
        'a_norm_post': gain((N_A_LAYERS, D_MODEL)),
        'a_w_in': normal((N_A_LAYERS, D_MODEL, GDN_IN), D_MODEL ** -0.5),
        'a_conv_w': normal((N_A_LAYERS, GDN_CONV, GDN_QKV), GDN_CONV ** -0.5),
        'a_log': a_log,
        'a_dt_bias': dt + jnp.log(-jnp.expm1(-dt)),
        'a_out_norm': gain((N_A_LAYERS, GDN_DV)),
        'a_w_out': normal((N_A_LAYERS, GDN_VAL, D_MODEL), GDN_VAL ** -0.5),
        'kv_norm': gain((D_MODEL,)),
        'kv_w_a': normal((D_MODEL, MLA_KV_LORA + MLA_ROPE), D_MODEL ** -0.5),
        'kv_a_norm': gain((MLA_KV_LORA,)),
        'kv_w_uk': normal((MLA_KV_LORA, MLA_HEADS, MLA_NOPE), MLA_KV_LORA ** -0.5),
        'kv_w_uv': normal((MLA_KV_LORA, MLA_HEADS, MLA_V), MLA_KV_LORA ** -0.5),
        'b_norm_pre': gain((N_B_LAYERS, D_MODEL)),
        'b_norm_post': gain((N_B_LAYERS, D_MODEL)),
        'b_w_q_a': normal((N_B_LAYERS, D_MODEL, MLA_Q_LORA), D_MODEL ** -0.5),
        'b_q_a_norm': gain((N_B_LAYERS, MLA_Q_LORA)),
        'b_w_q_b': normal((N_B_LAYERS, MLA_Q_LORA, MLA_HEADS * (MLA_NOPE + MLA_ROPE)), MLA_Q_LORA ** -0.5),
        'b_w_out': normal((N_B_LAYERS, MLA_HEADS * MLA_V, D_MODEL), (MLA_HEADS * MLA_V) ** -0.5),
        'f_norm_pre': gain((DEPTH, D_MODEL)),
        'f_norm_post': gain((DEPTH, D_MODEL)),
        'f_w_up': normal((DEPTH, D_MODEL, 2 * D_FF), D_MODEL ** -0.5),
        'f_conv_w': normal((DEPTH, FFN_CONV, 2 * D_FF), FFN_CONV ** -0.5),
        'f_conv_b': normal((DEPTH, 2 * D_FF), 0.01),
        'f_w_down': normal((DEPTH, D_FF, D_MODEL), D_FF ** -0.5),
    }


def reference(x_prompt, x_sample, state_delta_S, state_delta_conv, state_ffn_conv, cache_kv_latent, cache_k_rope,
              page_table, meta_tokens, a_norm_pre, a_norm_post, a_w_in, a_conv_w, a_log, a_dt_bias, a_out_norm,
              a_w_out, kv_norm, kv_w_a, kv_a_norm, kv_w_uk, kv_w_uv, b_norm_pre, b_norm_post, b_w_q_a, b_q_a_norm,
              b_w_q_b, b_w_out, f_norm_pre, f_norm_post, f_w_up, f_conv_w, f_conv_b, f_w_down):
    p = {
        'a_norm_pre': a_norm_pre, 'a_norm_post': a_norm_post, 'a_w_in': a_w_in, 'a_conv_w': a_conv_w,
        'a_log': a_log, 'a_dt_bias': a_dt_bias, 'a_out_norm': a_out_norm, 'a_w_out': a_w_out,
        'kv_norm': kv_norm, 'kv_w_a': kv_w_a, 'kv_a_norm': kv_a_norm, 'kv_w_uk': kv_w_uk, 'kv_w_uv': kv_w_uv,
        'b_norm_pre': b_norm_pre, 'b_norm_post': b_norm_post, 'b_w_q_a': b_w_q_a, 'b_q_a_norm': b_q_a_norm,
        'b_w_q_b': b_w_q_b, 'b_w_out': b_w_out,
        'f_norm_pre': f_norm_pre, 'f_norm_post': f_norm_post, 'f_w_up': f_w_up, 'f_conv_w': f_conv_w,
        'f_conv_b': f_conv_b, 'f_w_down': f_w_down,
    }
    bp = x_prompt.shape[0]
    L = N_META + x_prompt.shape[1]
    xp = jnp.concatenate([jnp.broadcast_to(meta_tokens.astype(x_prompt.dtype)[None], (bp, N_META, D_MODEL)),
                          x_prompt], axis=1)
    pos_p = jnp.arange(L, dtype=jnp.int32)
    segs_p = ((0, N_META, N_META), (N_META, L, GDN_CHUNK))
    zS = jnp.zeros((N_A_LAYERS, bp, GDN_HEADS, GDN_DK, GDN_DV), state_delta_S.dtype)
    zdc = jnp.zeros((N_A_LAYERS, bp, GDN_CONV - 1, GDN_QKV), x_prompt.dtype)
    zfc = jnp.zeros((DEPTH, bp, FFN_CONV - 1, 2 * D_FF), x_prompt.dtype)
    yp, p_delta_S, p_delta_conv, p_ffn_conv, p_kv_latent, p_k_rope = trunk(
        xp, pos_p, segs_p, zS, zdc, zfc, prompt_attend, p)

    bs, t_new = x_sample.shape[0], x_sample.shape[1]
    past_len = page_table.shape[1] * cache_kv_latent.shape[1]
    c_past = cache_kv_latent[page_table].reshape(bs, past_len, MLA_KV_LORA)
    k_past = cache_k_rope[page_table].reshape(bs, past_len, MLA_ROPE)
    pos_s = past_len + jnp.arange(t_new, dtype=jnp.int32)
    attend_s = functools.partial(sample_attend, c_past=c_past, k_past=k_past)
    ys, s_delta_S, s_delta_conv, s_ffn_conv, s_kv_latent, s_k_rope = trunk(
        x_sample, pos_s, ((0, t_new, t_new),), state_delta_S, state_delta_conv, state_ffn_conv, attend_s, p)

    y_prompt = yp[:, N_META:]
    return (y_prompt, ys, p_delta_S, p_delta_conv, p_ffn_conv, p_kv_latent, p_k_rope,
            s_delta_S, s_delta_conv, s_ffn_conv, s_kv_latent, s_k_rope)
```

```python
import functools
import math

import jax
import jax.numpy as jnp
from jax import lax
from jax.experimental import pallas as pl
from jax.experimental.pallas import tpu as pltpu

F32 = jnp.float32
BF16 = jnp.bfloat16

D_MODEL = 1024
N_META = 16
NORM_EPS = 1e-6

GDN_HEADS = 8
GDN_DK = 128
GDN_DV = 128
GDN_CONV = 4
GDN_CHUNK = 64
GDN_KEY = GDN_HEADS * GDN_DK
GDN_VAL = GDN_HEADS * GDN_DV
GDN_QKV = 2 * GDN_KEY + GDN_VAL

MLA_HEADS = 8
MLA_Q_LORA = 384
MLA_KV_LORA = 256
MLA_NOPE = 128
MLA_ROPE = 64
MLA_V = 128
MLA_SCALE = 1.0 / math.sqrt(MLA_NOPE + MLA_ROPE)
ROPE_THETA = 10000.0
MLA_QK = MLA_KV_LORA + 128

D_FF = 2816
FFN_CONV = 3
FFN_CHUNK = 256
FFN_NCH = D_FF // FFN_CHUNK

LANE = 128
SUBLANE = 8
BF16_ROWS = 16
VMEM_LIMIT = 56 * 1024 * 1024

HI = lax.Precision.HIGHEST


def _row_tile(n, cap, align):
    best = None
    for t in range(align, min(n, cap) + 1, align):
        if n % t == 0:
            best = t
    return n if best is None else best


def _cparams(sem):
    return pltpu.CompilerParams(dimension_semantics=sem, vmem_limit_bytes=VMEM_LIMIT)


def _rms(x, w):
    return x * lax.rsqrt(jnp.mean(x * x, axis=-1, keepdims=True) + NORM_EPS) * w


def _sigmoid(x):
    return 1.0 / (1.0 + jnp.exp(-x))


def _silu(x):
    return x * _sigmoid(x)


def _softplus(x):
    return jnp.maximum(x, 0.0) + jnp.log(1.0 + jnp.exp(-jnp.abs(x)))


def _dot(a, b):
    return jnp.dot(a, b, preferred_element_type=F32)


def _dot_nt(a, b, precision=None):
    return lax.dot_general(a, b, (((1,), (1,)), ((), ())), precision=precision, preferred_element_type=F32)


def _dot_hi(a, b):
    return jnp.dot(a, b, precision=HI, preferred_element_type=F32)


def _dot_tn_hi(a, b):
    return lax.dot_general(a, b, (((0,), (0,)), ((), ())), precision=HI, preferred_element_type=F32)


def _full(shape):
    nd = len(shape)
    return pl.BlockSpec(shape, lambda *_: (0,) * nd)


def _gdn_in_kernel(x_ref, g_ref, wqkv_ref, wz_ref, wab_ref, qkv_ref, z_ref, ab_ref):
    h = _rms(x_ref[0], g_ref[...]).astype(BF16)
    for n0 in range(0, GDN_QKV, 512):
        qkv_ref[0, :, n0:n0 + 512] = _dot(h, wqkv_ref[:, n0:n0 + 512])
    for n0 in range(0, GDN_VAL, 512):
        z_ref[0, :, n0:n0 + 512] = _dot(h, wz_ref[:, n0:n0 + 512])
    ab_ref[0] = _dot(h, wab_ref[...])


def _gdn_in(x, g, wqkv, wz, wab):
    B, L, D = x.shape
    tl = _row_tile(L, 400, SUBLANE)
    row = lambda n: pl.BlockSpec((1, tl, n), lambda b, l: (b, l, 0))
    return pl.pallas_call(
        _gdn_in_kernel,
        grid=(B, L // tl),
        in_specs=[row(D), _full((1, D)), _full((D, GDN_QKV)), _full((D, GDN_VAL)), _full((D, LANE))],
        out_specs=[row(GDN_QKV), row(GDN_VAL), row(LANE)],
        out_shape=[jax.ShapeDtypeStruct((B, L, GDN_QKV), F32), jax.ShapeDtypeStruct((B, L, GDN_VAL), F32),
                   jax.ShapeDtypeStruct((B, L, LANE), F32)],
        compiler_params=_cparams(("parallel", "parallel")),
        name="gdn_in",
    )(x, g, wqkv, wz, wab)


def _head_scalar(ref, sel):
    return jnp.sum(jnp.where(sel, ref[...], 0.0), axis=1, keepdims=True)


def _gdn_chunk(r0, C, S, first, refs, consts):
    (q_ref, k_ref, v_ref, wq_ref, wk_ref, wv_ref, pq_ref, pk_ref, pv_ref, ab_ref, z_ref, onorm_ref, o_ref, win_sc) = refs
    sel_a, sel_b, neg_ea, dtb = consts

    def window(src_ref, prev_ref, slot):
        if first:
            win_sc[slot, 0:SUBLANE, :] = jnp.zeros((SUBLANE, LANE), F32)
            win_sc[slot, SUBLANE - (GDN_CONV - 1):SUBLANE, :] = prev_ref[0]
            win_sc[slot, SUBLANE:SUBLANE + C, :] = src_ref[0, 0:C, :]
            return win_sc[slot, 0:SUBLANE + C, :]
        return src_ref[0, pl.ds(r0 - SUBLANE, C + SUBLANE), :]

    def conv(src_ref, prev_ref, w_ref, slot):
        win = window(src_ref, prev_ref, slot)
        w = w_ref[...]
        acc = win[SUBLANE:SUBLANE + C] * w[GDN_CONV - 1:GDN_CONV]
        for j in range(GDN_CONV - 1):
            off = SUBLANE - (GDN_CONV - 1) + j
            acc = acc + win[off:off + C] * w[j:j + 1]
        return _silu(acc)

    def l2n(t):
        return t * lax.rsqrt(jnp.sum(t * t, axis=-1, keepdims=True) + NORM_EPS)

    q = l2n(conv(q_ref, pq_ref, wq_ref, 0)) * (GDN_DK ** -0.5)
    k = l2n(conv(k_ref, pk_ref, wk_ref, 1))
    v = conv(v_ref, pv_ref, wv_ref, 2)

    ab = ab_ref[0, pl.ds(r0, C), :]
    a_col = jnp.sum(jnp.where(sel_a, ab, 0.0), axis=1, keepdims=True)
    b_col = jnp.sum(jnp.where(sel_b, ab, 0.0), axis=1, keepdims=True)
    g_col = neg_ea * _softplus(a_col + dtb)
    beta = _sigmoid(b_col)

    ri = lax.broadcasted_iota(jnp.int32, (C, C), 0)
    ci = lax.broadcasted_iota(jnp.int32, (C, C), 1)
    eye = ri == ci
    causal = ri >= ci
    g_row = jnp.sum(jnp.where(eye, jnp.broadcast_to(g_col, (C, C)), 0.0), axis=0, keepdims=True)
    gc_col = jnp.sum(jnp.where(causal, jnp.broadcast_to(g_row, (C, C)), 0.0), axis=1, keepdims=True)
    gc_row = jnp.sum(jnp.where(eye, jnp.broadcast_to(gc_col, (C, C)), 0.0), axis=0, keepdims=True)
    decay = jnp.where(causal, jnp.exp(jnp.where(causal, gc_col - gc_row, 0.0)), 0.0)
    g_last = gc_row[:, C - 1:C]
    eg = jnp.exp(gc_col)

    kb = k * beta
    A = jnp.where(ri > ci, _dot_nt(kb, k, HI) * decay, 0.0)
    X = jnp.concatenate([v * beta, kb * eg], axis=-1)
    P = -A
    X = X + _dot_hi(P, X)
    span = 2
    while span < C:
        P = _dot_hi(P, P)
        X = X + _dot_hi(P, X)
        span *= 2
    u = X[:, :GDN_DV]
    w = X[:, GDN_DV:]
    attn = _dot_nt(q, k, HI) * decay
    v_new = u - _dot_hi(w, S)
    o = _dot_hi(q * eg, S) + _dot_hi(attn, v_new)
    k_tail = k * jnp.exp(g_last - gc_col)
    S = S * jnp.exp(g_last) + _dot_tn_hi(k_tail, v_new)

    o = o * lax.rsqrt(jnp.mean(o * o, axis=-1, keepdims=True) + NORM_EPS) * onorm_ref[...]
    o = o * _silu(z_ref[0, pl.ds(r0, C), :])
    o_ref[0, pl.ds(r0, C), :] = o.astype(o_ref.dtype)
    return S


def _gdn_core_kernel(q_ref, k_ref, v_ref, wq_ref, wk_ref, wv_ref, pq_ref, pk_ref, pv_ref, ab_ref, z_ref,
                     alog_ref, dtb_ref, onorm_ref, s0_ref, o_ref, sout_ref, win_sc, *, n_chunks):
    hh = pl.program_id(1)
    lane = lax.broadcasted_iota(jnp.int32, (1, LANE), 1)
    sel_a = lane == hh
    sel_b = lane == hh + GDN_HEADS
    consts = (sel_a, sel_b, -jnp.exp(_head_scalar(alog_ref, sel_a)), _head_scalar(dtb_ref, sel_a))
    refs = (q_ref, k_ref, v_ref, wq_ref, wk_ref, wv_ref, pq_ref, pk_ref, pv_ref, ab_ref, z_ref, onorm_ref, o_ref, win_sc)

    S = _gdn_chunk(0, N_META, s0_ref[0, 0], True, refs, consts)

    def body(c, S):
        r0 = pl.multiple_of(N_META + c * GDN_CHUNK, BF16_ROWS)
        return _gdn_chunk(r0, GDN_CHUNK, S, False, refs, consts)

    S = lax.fori_loop(0, n_chunks, body, S)
    sout_ref[0, 0] = S


def _gdn_core(qkv, conv_w, conv_prev, ab, z, a_log, dt_bias, out_norm, S0):
    B, L, _ = qkv.shape
    n_chunks = (L - N_META) // GDN_CHUNK
    assert N_META + n_chunks * GDN_CHUNK == L
    nh = GDN_HEADS
    col = lambda off: pl.BlockSpec((1, L, LANE), lambda b, h: (b, 0, off + h))
    wcol = lambda off: pl.BlockSpec((GDN_CONV, LANE), lambda b, h: (0, off + h))
    pcol = lambda off: pl.BlockSpec((1, GDN_CONV - 1, LANE), lambda b, h: (b, 0, off + h))
    state = pl.BlockSpec((1, 1, GDN_DK, GDN_DV), lambda b, h: (b, h, 0, 0))
    return pl.pallas_call(
        functools.partial(_gdn_core_kernel, n_chunks=n_chunks),
        grid=(B, nh),
        in_specs=[col(0), col(nh), col(2 * nh), wcol(0), wcol(nh), wcol(2 * nh), pcol(0), pcol(nh), pcol(2 * nh),
                  pl.BlockSpec((1, L, LANE), lambda b, h: (b, 0, 0)), col(0),
                  _full((1, LANE)), _full((1, LANE)), _full((1, GDN_DV)), state],
        out_specs=[col(0), state],
        out_shape=[jax.ShapeDtypeStruct((B, L, GDN_VAL), BF16), jax.ShapeDtypeStruct((B, nh, GDN_DK, GDN_DV), F32)],
        scratch_shapes=[pltpu.VMEM((3, SUBLANE + N_META, LANE), F32)],
        compiler_params=_cparams(("parallel", "parallel")),
        name="gdn_core",
    )(qkv, qkv, qkv, conv_w, conv_w, conv_w, conv_prev, conv_prev, conv_prev, ab, z, a_log, dt_bias, out_norm, S0)


def _gdn_step_kernel(new_ref, prev_ref, w_ref, ab_ref, z_ref, alog_ref, dtb_ref, onorm_ref, s_ref, o_ref, sout_ref):
    w = w_ref[...]
    prev = prev_ref[0]
    y = new_ref[0] * w[GDN_CONV - 1:GDN_CONV]
    for j in range(GDN_CONV - 1):
        y = y + prev[j:j + 1] * w[j:j + 1]
    y = _silu(y)
    ab = ab_ref[0]
    z = z_ref[0]
    alog = alog_ref[...]
    dtb = dtb_ref[...]
    ri = lax.broadcasted_iota(jnp.int32, (GDN_DK, GDN_DK), 0)
    ci = lax.broadcasted_iota(jnp.int32, (GDN_DK, GDN_DK), 1)
    eye = ri == ci
    row8 = lax.broadcasted_iota(jnp.int32, (SUBLANE, GDN_DK), 0)

    def l2n(t):
        return t * lax.rsqrt(jnp.sum(t * t, axis=-1, keepdims=True) + NORM_EPS)

    for h in range(GDN_HEADS):
        q = l2n(y[:, h * GDN_DK:(h + 1) * GDN_DK]) * (GDN_DK ** -0.5)
        k = l2n(y[:, GDN_KEY + h * GDN_DK:GDN_KEY + (h + 1) * GDN_DK])
        v = y[:, 2 * GDN_KEY + h * GDN_DV:2 * GDN_KEY + (h + 1) * GDN_DV]
        g = -jnp.exp(alog[:, h:h + 1]) * _softplus(ab[:, h:h + 1] + dtb[:, h:h + 1])
        beta = _sigmoid(ab[:, GDN_HEADS + h:GDN_HEADS + h + 1])
        eg = jnp.exp(g)
        S = s_ref[0, h]
        kq = jnp.where(row8 == 0, jnp.broadcast_to(k, (SUBLANE, GDN_DK)),
                       jnp.where(row8 == 1, jnp.broadcast_to(q, (SUBLANE, GDN_DK)), 0.0))
        kqS = _dot_hi(kq, S)
        v_new = beta * (v - eg * kqS[0:1])
        qk = jnp.sum(q * k, axis=-1, keepdims=True)
        o = eg * kqS[1:2] + qk * v_new
        k_col = jnp.sum(jnp.where(eye, jnp.broadcast_to(k, (GDN_DK, GDN_DK)), 0.0), axis=1, keepdims=True)
        sout_ref[0, h] = S * eg + k_col * v_new
        o = o * lax.rsqrt(jnp.mean(o * o, axis=-1, keepdims=True) + NORM_EPS) * onorm_ref[...]
        o = o * _silu(z[:, h * GDN_DV:(h + 1) * GDN_DV])
        o_ref[0, :, h * GDN_DV:(h + 1) * GDN_DV] = o.astype(o_ref.dtype)


def _gdn_step(qkv_new, conv_prev, conv_w, ab, z, a_log, dt_bias, out_norm, S0):
    Bs = qkv_new.shape[0]
    per = lambda *s: pl.BlockSpec((1,) + s, lambda b: (b,) + (0,) * len(s))
    return pl.pallas_call(
        _gdn_step_kernel,
        grid=(Bs,),
        in_specs=[per(1, GDN_QKV), per(GDN_CONV - 1, GDN_QKV), _full((GDN_CONV, GDN_QKV)), per(1, LANE),
                  per(1, GDN_VAL), _full((1, LANE)), _full((1, LANE)), _full((1, GDN_DV)),
                  per(GDN_HEADS, GDN_DK, GDN_DV)],
        out_specs=[per(1, GDN_VAL), per(GDN_HEADS, GDN_DK, GDN_DV)],
        out_shape=[jax.ShapeDtypeStruct((Bs, 1, GDN_VAL), BF16),
                   jax.ShapeDtypeStruct((Bs, GDN_HEADS, GDN_DK, GDN_DV), F32)],
        compiler_params=_cparams(("parallel",)),
        name="gdn_step",
    )(qkv_new, conv_prev, conv_w, ab, z, a_log, dt_bias, out_norm, S0)


def _proj_residual_kernel(x_ref, o_ref, w_ref, g_ref, out_ref):
    y = _dot(o_ref[0], w_ref[...])
    out_ref[0] = x_ref[0] + _rms(y, g_ref[...])


def _proj_residual(x, o, w, g):
    B, L, D = x.shape
    K = o.shape[-1]
    tl = _row_tile(L, 700, BF16_ROWS)
    return pl.pallas_call(
        _proj_residual_kernel,
        grid=(B, L // tl),
        in_specs=[pl.BlockSpec((1, tl, D), lambda b, l: (b, l, 0)), pl.BlockSpec((1, tl, K), lambda b, l: (b, l, 0)),
                  _full((K, D)), _full((1, D))],
        out_specs=pl.BlockSpec((1, tl, D), lambda b, l: (b, l, 0)),
        out_shape=jax.ShapeDtypeStruct((B, L, D), F32),
        compiler_params=_cparams(("parallel", "parallel")),
        name="proj_residual",
    )(x, o, w, g)


def _ffn_chunk(j, u, tap1, tap0, cw_ref, cb_ref, wdn_ref, acc_sc):
    cw = cw_ref[j]
    conv = u * cw[2:3] + tap1 * cw[1:2] + tap0 * cw[0:1] + cb_ref[j]
    gate = conv[:, :FFN_CHUNK]
    val = conv[:, FFN_CHUNK:]
    acc_sc[...] += _dot((_silu(gate) * val).astype(BF16), wdn_ref[j])


def _ffn_seq_kernel(x_ref, gpre_ref, wup_ref, cw_ref, cb_ref, wdn_ref, gpost_ref, prev_ref, out_ref, newprev_ref,
                    h_sc, acc_sc, u_sc, carry_sc, *, tl):
    x = x_ref[0]
    h_sc[...] = _rms(x, gpre_ref[...]).astype(BF16)
    acc_sc[...] = jnp.zeros_like(acc_sc)

    @pl.when(pl.program_id(1) == 0)
    def _():
        carry_sc[...] = jnp.zeros_like(carry_sc)
        carry_sc[:, SUBLANE - (FFN_CONV - 1):SUBLANE, :] = prev_ref[0]

    def body(j, carry):
        u = _dot(h_sc[...], wup_ref[j])
        u_sc[0:SUBLANE, :] = carry_sc[j]
        u_sc[SUBLANE:SUBLANE + tl, :] = u
        carry_sc[j] = u[tl - SUBLANE:tl]
        newprev_ref[0, j] = u[tl - (FFN_CONV - 1):tl]
        _ffn_chunk(j, u, u_sc[SUBLANE - 1:SUBLANE - 1 + tl, :], u_sc[SUBLANE - 2:SUBLANE - 2 + tl, :],
                   cw_ref, cb_ref, wdn_ref, acc_sc)
        return carry

    lax.fori_loop(0, FFN_NCH, body, 0)
    out_ref[0] = x + _rms(acc_sc[...], gpost_ref[...])


def _ffn_step_kernel(x_ref, gpre_ref, wup_ref, cw_ref, cb_ref, wdn_ref, gpost_ref, prev_ref, out_ref, u_ref,
                     h_sc, acc_sc):
    x = x_ref[0]
    h_sc[...] = _rms(x, gpre_ref[...]).astype(BF16)
    acc_sc[...] = jnp.zeros_like(acc_sc)

    def body(j, carry):
        u = _dot(h_sc[...], wup_ref[j])
        u_ref[j] = u
        _ffn_chunk(j, u, prev_ref[j, 1], prev_ref[j, 0], cw_ref, cb_ref, wdn_ref, acc_sc)
        return carry

    lax.fori_loop(0, FFN_NCH, body, 0)
    out_ref[0] = x + _rms(acc_sc[...], gpost_ref[...])


def _ffn_weight_specs():
    w2 = 2 * FFN_CHUNK
    return [_full((1, D_MODEL)), _full((FFN_NCH, D_MODEL, w2)), _full((FFN_NCH, FFN_CONV, w2)),
            _full((FFN_NCH, 1, w2)), _full((FFN_NCH, FFN_CHUNK, D_MODEL)), _full((1, D_MODEL))]


def _chunk_cols(t):
    lead = t.shape[:-1]
    t = t.reshape(lead + (2, FFN_NCH, FFN_CHUNK))
    t = jnp.moveaxis(t, -2, 0)
    return t.reshape((FFN_NCH,) + lead + (2 * FFN_CHUNK,))


def _unchunk_cols(t):
    lead = t.shape[1:-1]
    t = t.reshape((FFN_NCH,) + lead + (2, FFN_CHUNK))
    t = jnp.moveaxis(t, 0, -2)
    return t.reshape(lead + (2 * D_FF,))


def _ffn_seq(x, prev, wts):
    B, L, D = x.shape
    tl = _row_tile(L, 400, SUBLANE)
    w2 = 2 * FFN_CHUNK
    prev_c = jnp.moveaxis(_chunk_cols(prev), 0, 1)
    out, newprev = pl.pallas_call(
        functools.partial(_ffn_seq_kernel, tl=tl),
        grid=(B, L // tl),
        in_specs=[pl.BlockSpec((1, tl, D), lambda b, l: (b, l, 0))] + _ffn_weight_specs()
        + [pl.BlockSpec((1, FFN_NCH, FFN_CONV - 1, w2), lambda b, l: (b, 0, 0, 0))],
        out_specs=[pl.BlockSpec((1, tl, D), lambda b, l: (b, l, 0)),
                   pl.BlockSpec((1, FFN_NCH, FFN_CONV - 1, w2), lambda b, l: (b, 0, 0, 0))],
        out_shape=[jax.ShapeDtypeStruct((B, L, D), F32), jax.ShapeDtypeStruct((B, FFN_NCH, FFN_CONV - 1, w2), F32)],
        scratch_shapes=[pltpu.VMEM((tl, D), BF16), pltpu.VMEM((tl, D), F32), pltpu.VMEM((SUBLANE + tl, w2), F32),
                        pltpu.VMEM((FFN_NCH, SUBLANE, w2), F32)],
        compiler_params=_cparams(("parallel", "arbitrary")),
        name="ffn_seq",
    )(x, *wts, prev_c)
    return out, _unchunk_cols(jnp.moveaxis(newprev, 1, 0))


def _ffn_step(x, prev, wts):
    _, Bs, D = x.shape
    w2 = 2 * FFN_CHUNK
    prev_c = _chunk_cols(jnp.moveaxis(prev, 1, 0))
    out, u = pl.pallas_call(
        _ffn_step_kernel,
        grid=(1,),
        in_specs=[_full((1, Bs, D))] + _ffn_weight_specs() + [_full((FFN_NCH, FFN_CONV - 1, Bs, w2))],
        out_specs=[_full((1, Bs, D)), _full((FFN_NCH, Bs, w2))],
        out_shape=[jax.ShapeDtypeStruct((1, Bs, D), F32), jax.ShapeDtypeStruct((FFN_NCH, Bs, w2), F32)],
        scratch_shapes=[pltpu.VMEM((Bs, D), BF16), pltpu.VMEM((Bs, D), F32)],
        compiler_params=_cparams(("arbitrary",)),
        name="ffn_step",
    )(x, *wts, prev_c)
    return out, jnp.concatenate([prev[:, 1:], _unchunk_cols(u)[:, None]], axis=1)


def _rot_half(t, width):
    lane = lax.broadcasted_iota(jnp.int32, t.shape, t.ndim - 1)
    first = (lane % MLA_ROPE) < (MLA_ROPE // 2)
    return jnp.where(first, pltpu.roll(t, width - MLA_ROPE // 2, t.ndim - 1), pltpu.roll(t, MLA_ROPE // 2, t.ndim - 1))


def _mla_proj_kernel(x_ref, gkv_ref, wkva_ref, gkva_ref, gq_ref, wqa_ref, gqa_ref, wqb_ref, wuk_ref, cs_ref,
                     c_ref, kr_ref, kcat_ref, q_ref):
    x = x_ref[0]
    cos = cs_ref[:, 0:LANE]
    sin = cs_ref[:, LANE:2 * LANE]
    lane = lax.broadcasted_iota(jnp.int32, (1, LANE), 1)

    ckv = _dot(_rms(x, gkv_ref[...]).astype(BF16), wkva_ref[...])
    c = _rms(ckv[:, :MLA_KV_LORA], gkva_ref[...])
    r = ckv[:, MLA_KV_LORA:MLA_QK]
    kr = r * cos + _rot_half(r, LANE) * sin
    c_ref[0] = c
    kr_ref[0] = kr[:, :MLA_ROPE]
    kcat_ref[0, :, 0:MLA_KV_LORA] = c.astype(BF16)
    kcat_ref[0, :, MLA_KV_LORA:MLA_QK] = kr.astype(BF16)

    qa = _dot(_rms(x, gq_ref[...]).astype(BF16), wqa_ref[...])
    qb = _dot(_rms(qa, gqa_ref[...]).astype(BF16), wqb_ref[...])
    n_nope = MLA_HEADS * MLA_NOPE
    n_pe = MLA_HEADS * MLA_ROPE
    pe = qb[:, n_nope:n_nope + n_pe]
    reps = n_pe // LANE
    pe = pe * jnp.concatenate([cos] * reps, axis=1) + _rot_half(pe, n_pe) * jnp.concatenate([sin] * reps, axis=1)
    for h in range(MLA_HEADS):
        q_lat = _dot(qb[:, h * MLA_NOPE:(h + 1) * MLA_NOPE].astype(BF16), wuk_ref[h])
        q_ref[0, h, :, 0:MLA_KV_LORA] = q_lat.astype(BF16)
        t = pe[:, (h // 2) * LANE:(h // 2 + 1) * LANE]
        if h % 2 == 1:
            t = pltpu.roll(t, MLA_ROPE, 1)
        q_ref[0, h, :, MLA_KV_LORA:MLA_QK] = jnp.where(lane < MLA_ROPE, t, 0.0).astype(BF16)


def _mla_proj(x, cs, gkv, wkva, gkva, gq, wqa, gqa, wqb, wuk):
    B, L, D = x.shape
    tl = _row_tile(L, 700, BF16_ROWS)
    row = lambda n: pl.BlockSpec((1, tl, n), lambda b, l: (b, l, 0))
    return pl.pallas_call(
        _mla_proj_kernel,
        grid=(B, L // tl),
        in_specs=[row(D), _full((1, D)), _full((D, MLA_QK)), _full((1, MLA_KV_LORA)), _full((1, D)),
                  _full((D, MLA_Q_LORA)), _full((1, MLA_Q_LORA)), _full(wqb.shape), _full(wuk.shape),
                  pl.BlockSpec((tl, 2 * LANE), lambda b, l: (l, 0))],
        out_specs=[row(MLA_KV_LORA), row(MLA_ROPE), row(MLA_QK),
                   pl.BlockSpec((1, MLA_HEADS, tl, MLA_QK), lambda b, l: (b, 0, l, 0))],
        out_shape=[jax.ShapeDtypeStruct((B, L, MLA_KV_LORA), F32), jax.ShapeDtypeStruct((B, L, MLA_ROPE), F32),
                   jax.ShapeDtypeStruct((B, L, MLA_QK), BF16),
                   jax.ShapeDtypeStruct((B, MLA_HEADS, L, MLA_QK), BF16)],
        compiler_params=_cparams(("parallel", "parallel")),
        name="mla_proj",
    )(x, gkv, wkva, gkva, gq, wqa, gqa, wqb, wuk, cs)


def _attn_kernel(q_ref, kcat_ref, o_ref, m_sc, l_sc, acc_sc, *, tq):
    i = pl.program_id(1)
    q = q_ref[0, 0]
    m_sc[...] = jnp.full_like(m_sc, -jnp.inf)
    l_sc[...] = jnp.zeros_like(l_sc)
    acc_sc[...] = jnp.zeros_like(acc_sc)
    ri = lax.broadcasted_iota(jnp.int32, (tq, tq), 0)
    ci = lax.broadcasted_iota(jnp.int32, (tq, tq), 1)

    def body(kt, carry):
        kv = kcat_ref[0, pl.ds(pl.multiple_of(kt * tq, BF16_ROWS), tq), :]
        s = _dot_nt(q, kv) * MLA_SCALE
        s = jnp.where(kt * tq + ci <= i * tq + ri, s, -jnp.inf)
        m_prev = m_sc[...]
        m_new = jnp.maximum(m_prev, jnp.max(s, axis=1, keepdims=True))
        alpha = jnp.exp(m_prev - m_new)
        p = jnp.exp(s - m_new)
        l_sc[...] = alpha * l_sc[...] + jnp.sum(p, axis=1, keepdims=True)
        acc_sc[...] = alpha * acc_sc[...] + _dot(p.astype(BF16), kv[:, :MLA_KV_LORA])
        m_sc[...] = m_new
        return carry

    lax.fori_loop(0, i + 1, body, 0)
    o_ref[0, 0] = (acc_sc[...] / l_sc[...]).astype(o_ref.dtype)


def _attn_seq(q, kcat):
    B, nh, L, _ = q.shape
    tq = _row_tile(L, 700, BF16_ROWS)
    return pl.pallas_call(
        functools.partial(_attn_kernel, tq=tq),
        grid=(B, L // tq, nh),
        in_specs=[pl.BlockSpec((1, 1, tq, MLA_QK), lambda b, i, h: (b, h, i, 0)),
                  pl.BlockSpec((1, L, MLA_QK), lambda b, i, h: (b, 0, 0))],
        out_specs=pl.BlockSpec((1, 1, tq, MLA_KV_LORA), lambda b, i, h: (b, h, i, 0)),
        out_shape=jax.ShapeDtypeStruct((B, nh, L, MLA_KV_LORA), BF16),
        scratch_shapes=[pltpu.VMEM((tq, 1), F32), pltpu.VMEM((tq, 1), F32), pltpu.VMEM((tq, MLA_KV_LORA), F32)],
        compiler_params=_cparams(("parallel", "parallel", "parallel")),
        name="attn_seq",
    )(q, kcat)


def _attn_paged_kernel(pt_ref, q_ref, kcat_ref, cache_c_ref, cache_k_ref, o_ref, cbuf, kbuf, c16, s_sc, sem,
                       *, n_pages, page, kchunk):
    b = pl.program_id(0)
    nb = pl.num_programs(0)
    P = n_pages * page

    def copies(bb, slot):
        out = []
        for pg in range(n_pages):
            idx = pt_ref[bb, pg]
            out.append(pltpu.make_async_copy(cache_c_ref.at[idx], cbuf.at[slot, pl.ds(pg * page, page), :],
                                             sem.at[0, slot]))
            out.append(pltpu.make_async_copy(cache_k_ref.at[idx], kbuf.at[slot, pl.ds(pg * page, page), :],
                                             sem.at[1, slot]))
        return out

    @pl.when(b == 0)
    def _():
        for cp in copies(0, 0):
            cp.start()

    @pl.when(b + 1 < nb)
    def _():
        for cp in copies(b + 1, (b + 1) % 2):
            cp.start()

    slot = b % 2
    for cp in copies(b, slot):
        cp.wait()

    q = q_ref[0]
    q_lat = q[:, :MLA_KV_LORA]
    q_pe = q[:, MLA_KV_LORA:MLA_KV_LORA + MLA_ROPE]
    for c0 in range(0, P, kchunk):
        cc = cbuf[slot, c0:c0 + kchunk, :].astype(BF16)
        c16[c0:c0 + kchunk, :] = cc
        kk = kbuf[slot, c0:c0 + kchunk, :].astype(BF16)
        s_sc[:, c0:c0 + kchunk] = (_dot_nt(q_lat, cc) + _dot_nt(q_pe, kk)) * MLA_SCALE
    kself = kcat_ref[0]
    s_self = jnp.sum(q.astype(F32) * kself.astype(F32), axis=-1, keepdims=True) * MLA_SCALE
    s = s_sc[...]
    m = jnp.maximum(jnp.max(s, axis=-1, keepdims=True), s_self)
    p = jnp.exp(s - m)
    p_self = jnp.exp(s_self - m)
    denom = jnp.sum(p, axis=-1, keepdims=True) + p_self
    s_sc[...] = p
    acc = p_self * kself[:, :MLA_KV_LORA].astype(F32)
    for c0 in range(0, P, kchunk):
        acc = acc + _dot(s_sc[:, c0:c0 + kchunk].astype(BF16), c16[c0:c0 + kchunk, :])
    o_ref[0] = (acc / denom).astype(o_ref.dtype)


def _attn_paged(q, kcat, cache_c, cache_k, page_table):
    Bs, nh, _ = q.shape
    n_pages = page_table.shape[1]
    page = cache_c.shape[1]
    P = n_pages * page
    kchunk = _row_tile(P, 1024, LANE)
    grid_spec = pltpu.PrefetchScalarGridSpec(
        num_scalar_prefetch=1,
        grid=(Bs,),
        in_specs=[pl.BlockSpec((1, nh, MLA_QK), lambda b, pt: (b, 0, 0)),
                  pl.BlockSpec((1, 1, MLA_QK), lambda b, pt: (b, 0, 0)),
                  pl.BlockSpec(memory_space=pl.ANY), pl.BlockSpec(memory_space=pl.ANY)],
        out_specs=pl.BlockSpec((1, nh, MLA_KV_LORA), lambda b, pt: (b, 0, 0)),
        scratch_shapes=[pltpu.VMEM((2, P, MLA_KV_LORA), F32), pltpu.VMEM((2, P, MLA_ROPE), F32),
                        pltpu.VMEM((P, MLA_KV_LORA), BF16), pltpu.VMEM((nh, P), F32),
                        pltpu.SemaphoreType.DMA((2, 2))],
    )
    return pl.pallas_call(
        functools.partial(_attn_paged_kernel, n_pages=n_pages, page=page, kchunk=kchunk),
        grid_spec=grid_spec,
        out_shape=jax.ShapeDtypeStruct((Bs, nh, MLA_KV_LORA), BF16),
        compiler_params=_cparams(("arbitrary",)),
        name="attn_paged",
    )(page_table, q, kcat, cache_c, cache_k)


def _mla_out_kernel(x_ref, o_ref, wuv_ref, wout_ref, g_ref, out_ref, acc_sc):
    for h in range(MLA_HEADS):
        oh = _dot(o_ref[0, h], wuv_ref[h]).astype(BF16)
        y = _dot(oh, wout_ref[h])
        if h == 0:
            acc_sc[...] = y
        else:
            acc_sc[...] += y
    out_ref[0] = x_ref[0] + _rms(acc_sc[...], g_ref[...])


def _mla_out(x, o_lat, wuv, wout, g):
    B, L, D = x.shape
    tl = _row_tile(L, 700, BF16_ROWS)
    return pl.pallas_call(
        _mla_out_kernel,
        grid=(B, L // tl),
        in_specs=[pl.BlockSpec((1, tl, D), lambda b, l: (b, l, 0)),
                  pl.BlockSpec((1, MLA_HEADS, tl, MLA_KV_LORA), lambda b, l: (b, 0, l, 0)),
                  _full(wuv.shape), _full(wout.shape), _full((1, D))],
        out_specs=pl.BlockSpec((1, tl, D), lambda b, l: (b, l, 0)),
        out_shape=jax.ShapeDtypeStruct((B, L, D), F32),
        scratch_shapes=[pltpu.VMEM((tl, D), F32)],
        compiler_params=_cparams(("parallel", "parallel")),
        name="mla_out",
    )(x, o_lat, wuv, wout, g)


def _rope_table(positions):
    half = MLA_ROPE // 2
    inv = ROPE_THETA ** (-jnp.arange(half, dtype=F32) / half)
    ang = positions.astype(F32)[:, None] * inv[None, :]
    cos, sin = jnp.cos(ang), jnp.sin(ang)
    reps = LANE // MLA_ROPE
    return jnp.concatenate([jnp.concatenate([cos, cos] * reps, axis=1), jnp.concatenate([-sin, sin] * reps, axis=1)],
                           axis=1)


def _row(v, width=None):
    v = v.reshape(1, -1).astype(F32)
    if width is not None and v.shape[1] < width:
        v = jnp.pad(v, ((0, 0), (0, width - v.shape[1])))
    return v


def _prepare(p):
    i = 0
    w_in = p['a_w_in'][i]
    n_ab = 2 * GDN_HEADS
    wq_b = p['b_w_q_b'][i].reshape(MLA_Q_LORA, MLA_HEADS, MLA_NOPE + MLA_ROPE)
    wq_b = jnp.concatenate([wq_b[:, :, :MLA_NOPE].reshape(MLA_Q_LORA, -1), wq_b[:, :, MLA_NOPE:].reshape(MLA_Q_LORA, -1)],
                           axis=1)
    prep = {
        'a_norm_pre': _row(p['a_norm_pre'][i]), 'a_norm_post': _row(p['a_norm_post'][i]),
        'a_wqkv': w_in[:, :GDN_QKV].astype(BF16), 'a_wz': w_in[:, GDN_QKV:GDN_QKV + GDN_VAL].astype(BF16),
        'a_wab': jnp.pad(w_in[:, GDN_QKV + GDN_VAL:], ((0, 0), (0, LANE - n_ab))).astype(BF16),
        'a_conv_w': p['a_conv_w'][i].astype(F32), 'a_log': _row(p['a_log'][i], LANE),
        'a_dt_bias': _row(p['a_dt_bias'][i], LANE), 'a_out_norm': _row(p['a_out_norm'][i]),
        'a_w_out': p['a_w_out'][i].astype(BF16),
        'kv_norm': _row(p['kv_norm']),
        'kv_w_a': jnp.pad(p['kv_w_a'], ((0, 0), (0, MLA_QK - MLA_KV_LORA - MLA_ROPE))).astype(BF16),
        'kv_a_norm': _row(p['kv_a_norm']),
        'kv_w_uk': jnp.transpose(p['kv_w_uk'], (1, 2, 0)).astype(BF16),
        'kv_w_uv': jnp.transpose(p['kv_w_uv'], (1, 0, 2)).astype(BF16),
        'b_norm_pre': _row(p['b_norm_pre'][i]), 'b_norm_post': _row(p['b_norm_post'][i]),
        'b_w_q_a': p['b_w_q_a'][i].astype(BF16), 'b_q_a_norm': _row(p['b_q_a_norm'][i]),
        'b_w_q_b': wq_b.astype(BF16),
        'b_w_out': p['b_w_out'][i].reshape(MLA_HEADS, MLA_V, D_MODEL).astype(BF16),
        'ffn': [],
    }
    for layer in range(2):
        w_up = jnp.moveaxis(_chunk_cols(p['f_w_up'][layer]), 0, 0).astype(BF16)
        prep['ffn'].append((
            _row(p['f_norm_pre'][layer]), w_up, _chunk_cols(p['f_conv_w'][layer]).astype(F32),
            _chunk_cols(p['f_conv_b'][layer].reshape(1, -1)).astype(F32),
            p['f_w_down'][layer].reshape(FFN_NCH, FFN_CHUNK, D_MODEL).astype(BF16), _row(p['f_norm_post'][layer])))
    return prep


def _mla_block(x, cs, prep):
    return _mla_proj(x, cs, prep['kv_norm'], prep['kv_w_a'], prep['kv_a_norm'], prep['b_norm_pre'], prep['b_w_q_a'],
                     prep['b_q_a_norm'], prep['b_w_q_b'], prep['kv_w_uk'])


def _trunk_seq(x, S0, dconv0, fconv0, prep):
    B, L, _ = x.shape
    qkv, z, ab = _gdn_in(x, prep['a_norm_pre'], prep['a_wqkv'], prep['a_wz'], prep['a_wab'])
    o, S = _gdn_core(qkv, prep['a_conv_w'], dconv0[0], ab, z, prep['a_log'], prep['a_dt_bias'], prep['a_out_norm'], S0[0])
    dconv = jnp.concatenate([dconv0[0], qkv], axis=1)[:, L:]
    x = _proj_residual(x, o, prep['a_w_out'], prep['a_norm_post'])
    x, fconv_a = _ffn_seq(x, fconv0[0], prep['ffn'][0])
    cs = _rope_table(jnp.arange(L, dtype=jnp.int32))
    c, kr, kcat, q = _mla_block(x, cs, prep)
    o_lat = _attn_seq(q, kcat)
    x = _mla_out(x, o_lat, prep['kv_w_uv'], prep['b_w_out'], prep['b_norm_post'])
    x, fconv_b = _ffn_seq(x, fconv0[1], prep['ffn'][1])
    return x, S[None], dconv[None], jnp.stack([fconv_a, fconv_b]), c, kr


def _trunk_step(x, pos, S0, dconv0, fconv0, cache_c, cache_k, page_table, prep):
    Bs = x.shape[0]
    xt = x.reshape(1, Bs, D_MODEL)
    qkv, z, ab = _gdn_in(xt, prep['a_norm_pre'], prep['a_wqkv'], prep['a_wz'], prep['a_wab'])
    o, S = _gdn_step(qkv.reshape(Bs, 1, -1), dconv0[0], prep['a_conv_w'], ab.reshape(Bs, 1, -1), z.reshape(Bs, 1, -1),
                     prep['a_log'], prep['a_dt_bias'], prep['a_out_norm'], S0[0])
    dconv = jnp.concatenate([dconv0[0][:, 1:], qkv.reshape(Bs, 1, -1)], axis=1)
    xt = _proj_residual(xt, o.reshape(1, Bs, -1), prep['a_w_out'], prep['a_norm_post'])
    xt, fconv_a = _ffn_step(xt, fconv0[0], prep['ffn'][0])
    cs = _rope_table(jnp.full((Bs,), pos, jnp.int32))
    c, kr, kcat, q = _mla_block(xt, cs, prep)
    q = jnp.moveaxis(q[0], 0, 1)
    o_lat = _attn_paged(q, kcat.reshape(Bs, 1, MLA_QK), cache_c, cache_k, page_table)
    o_lat = jnp.moveaxis(o_lat, 0, 1)[None]
    xt = _mla_out(xt, o_lat, prep['kv_w_uv'], prep['b_w_out'], prep['b_norm_post'])
    xt, fconv_b = _ffn_step(xt, fconv0[1], prep['ffn'][1])
    return (xt.reshape(Bs, 1, D_MODEL), S[None], dconv[None], jnp.stack([fconv_a, fconv_b]),
            c.reshape(Bs, 1, MLA_KV_LORA), kr.reshape(Bs, 1, MLA_ROPE))


def kernel(x_prompt, x_sample, state_delta_S, state_delta_conv, state_ffn_conv, cache_kv_latent, cache_k_rope,
           page_table, meta_tokens, a_norm_pre, a_norm_post, a_w_in, a_conv_w, a_log, a_dt_bias, a_out_norm, a_w_out,
           kv_norm, kv_w_a, kv_a_norm, kv_w_uk, kv_w_uv, b_norm_pre, b_norm_post, b_w_q_a, b_q_a_norm, b_w_q_b,
           b_w_out, f_norm_pre, f_norm_post, f_w_up, f_conv_w, f_conv_b, f_w_down):
    prep = _prepare({
        'a_norm_pre': a_norm_pre, 'a_norm_post': a_norm_post, 'a_w_in': a_w_in, 'a_conv_w': a_conv_w, 'a_log': a_log,
        'a_dt_bias': a_dt_bias, 'a_out_norm': a_out_norm, 'a_w_out': a_w_out, 'kv_norm': kv_norm, 'kv_w_a': kv_w_a,
        'kv_a_norm': kv_a_norm, 'kv_w_uk': kv_w_uk, 'kv_w_uv': kv_w_uv, 'b_norm_pre': b_norm_pre,
        'b_norm_post': b_norm_post, 'b_w_q_a': b_w_q_a, 'b_q_a_norm': b_q_a_norm, 'b_w_q_b': b_w_q_b,
        'b_w_out': b_w_out, 'f_norm_pre': f_norm_pre, 'f_norm_post': f_norm_post, 'f_w_up': f_w_up,
        'f_conv_w': f_conv_w, 'f_conv_b': f_conv_b, 'f_w_down': f_w_down})

    bp = x_prompt.shape[0]
    xp = jnp.concatenate([jnp.broadcast_to(meta_tokens.astype(x_prompt.dtype)[None], (bp, N_META, D_MODEL)), x_prompt],
                         axis=1)
    zS = jnp.zeros((1, bp, GDN_HEADS, GDN_DK, GDN_DV), state_delta_S.dtype)
    zdc = jnp.zeros((1, bp, GDN_CONV - 1, GDN_QKV), x_prompt.dtype)
    zfc = jnp.zeros((2, bp, FFN_CONV - 1, 2 * D_FF), x_prompt.dtype)
    yp, p_S, p_dconv, p_fconv, p_c, p_kr = _trunk_seq(xp, zS, zdc, zfc, prep)

    past_len = page_table.shape[1] * cache_kv_latent.shape[1]
    ys, s_S, s_dconv, s_fconv, s_c, s_kr = _trunk_step(x_sample, past_len, state_delta_S, state_delta_conv,
                                                       state_ffn_conv, cache_kv_latent, cache_k_rope, page_table, prep)
    return (yp[:, N_META:], ys, p_S, p_dconv, p_fconv, p_c, p_kr, s_S, s_dconv, s_fconv, s_c, s_kr)
```

```python
import functools
import math

import jax
import jax.numpy as jnp
from jax import lax
from jax.experimental import pallas as pl
from jax.experimental.pallas import tpu as pltpu

F32 = jnp.float32
BF16 = jnp.bfloat16

D_MODEL = 1024
N_META = 16
NORM_EPS = 1e-6

GDN_HEADS = 8
GDN_DK = 128
GDN_DV = 128
GDN_CONV = 4
GDN_CHUNK = 64
GDN_KEY = GDN_HEADS * GDN_DK
GDN_VAL = GDN_HEADS * GDN_DV
GDN_QKV = 2 * GDN_KEY + GDN_VAL

MLA_HEADS = 8
MLA_Q_LORA = 384
MLA_KV_LORA = 256
MLA_NOPE = 128
MLA_ROPE = 64
MLA_V = 128
MLA_SCALE = 1.0 / math.sqrt(MLA_NOPE + MLA_ROPE)
ROPE_THETA = 10000.0
MLA_QK = MLA_KV_LORA + 128

D_FF = 2816
FFN_CONV = 3
FFN_CHUNK = 256
FFN_NCH = D_FF // FFN_CHUNK

LANE = 128
SUBLANE = 8
BF16_ROWS = 16
VMEM_LIMIT = 56 * 1024 * 1024

HI = lax.Precision.HIGHEST


def _row_tile(n, cap, align):
    best = None
    for t in range(align, min(n, cap) + 1, align):
        if n % t == 0:
            best = t
    return n if best is None else best


def _cparams(sem):
    return pltpu.CompilerParams(dimension_semantics=sem, vmem_limit_bytes=VMEM_LIMIT)


def _rms(x, w):
    return x * lax.rsqrt(jnp.mean(x * x, axis=-1, keepdims=True) + NORM_EPS) * w


def _sigmoid(x):
    return 1.0 / (1.0 + jnp.exp(-x))


def _silu(x):
    return x * _sigmoid(x)


def _softplus(x):
    return jnp.maximum(x, 0.0) + jnp.log(1.0 + jnp.exp(-jnp.abs(x)))


def _dot(a, b):
    return jnp.dot(a, b, preferred_element_type=F32)


def _dot_nt(a, b, precision=None):
    return lax.dot_general(a, b, (((1,), (1,)), ((), ())), precision=precision, preferred_element_type=F32)


def _dot_hi(a, b):
    return jnp.dot(a, b, precision=HI, preferred_element_type=F32)


def _full(shape, pipeline_mode=None):
    nd = len(shape)
    return pl.BlockSpec(shape, lambda *_: (0,) * nd, pipeline_mode=pipeline_mode)


def _gdn_in_kernel(x_ref, g_ref, wqkv_ref, wz_ref, wab_ref, qkv_ref, z_ref, ab_ref, tail_ref, *, tl):
    h = _rms(x_ref[0], g_ref[...]).astype(BF16)
    for n0 in range(0, GDN_QKV, 512):
        r = _dot(h, wqkv_ref[:, n0:n0 + 512])
        qkv_ref[0, :, n0:n0 + 512] = r.astype(qkv_ref.dtype)
        tail_ref[0, :, n0:n0 + 512] = r[tl - SUBLANE:tl]
    for n0 in range(0, GDN_VAL, 512):
        z_ref[0, :, n0:n0 + 512] = _dot(h, wz_ref[:, n0:n0 + 512]).astype(z_ref.dtype)
    ab_ref[0] = _dot(h, wab_ref[...])


def _gdn_in(x, g, wqkv, wz, wab, act_dtype):
    B, L, D = x.shape
    tl = _row_tile(L, 700, BF16_ROWS)
    row = lambda n: pl.BlockSpec((1, tl, n), lambda b, l: (b, l, 0))
    return pl.pallas_call(
        functools.partial(_gdn_in_kernel, tl=tl),
        grid=(B, L // tl),
        in_specs=[row(D), _full((1, D)), _full((D, GDN_QKV)), _full((D, GDN_VAL)), _full((D, LANE))],
        out_specs=[row(GDN_QKV), row(GDN_VAL), row(LANE), pl.BlockSpec((1, SUBLANE, GDN_QKV), lambda b, l: (b, 0, 0))],
        out_shape=[jax.ShapeDtypeStruct((B, L, GDN_QKV), act_dtype), jax.ShapeDtypeStruct((B, L, GDN_VAL), act_dtype),
                   jax.ShapeDtypeStruct((B, L, LANE), F32), jax.ShapeDtypeStruct((B, SUBLANE, GDN_QKV), F32)],
        compiler_params=_cparams(("parallel", "arbitrary")),
        name="gdn_in",
    )(x, g, wqkv, wz, wab)


def _head_scalar(ref, sel):
    return jnp.sum(jnp.where(sel, ref[...], 0.0), axis=1, keepdims=True)


def _interleave(gens):
    results = [None] * len(gens)
    active = list(range(len(gens)))
    while active:
        for i in list(active):
            try:
                next(gens[i])
            except StopIteration as stop:
                results[i] = stop.value
                active.remove(i)
    return results


def _gdn_prepare_chunk(r0, C, first, hp, refs, consts):
    (q_ref, k_ref, v_ref, wq_ref, wk_ref, wv_ref, pq_ref, pk_ref, pv_ref, ab_ref, win_sc) = refs
    sel_a, sel_b, neg_ea, dtb = consts[hp]
    lanes = slice(hp * LANE, (hp + 1) * LANE)
    halo = BF16_ROWS

    def window(src_ref, prev_ref, slot):
        if first:
            win_sc[slot, 0:halo, :] = jnp.zeros((halo, LANE), F32)
            win_sc[slot, halo - (GDN_CONV - 1):halo, :] = prev_ref[0, :, lanes]
            win_sc[slot, halo:halo + C, :] = src_ref[0, 0:C, lanes].astype(F32)
            return win_sc[slot, 0:halo + C, :]
        return src_ref[0, pl.ds(r0 - halo, C + halo), lanes].astype(F32)

    def conv(src_ref, prev_ref, w_ref, slot):
        win = window(src_ref, prev_ref, slot)
        w = w_ref[:, lanes]
        acc = win[halo:halo + C] * w[GDN_CONV - 1:GDN_CONV]
        for j in range(GDN_CONV - 1):
            off = halo - (GDN_CONV - 1) + j
            acc = acc + win[off:off + C] * w[j:j + 1]
        return _silu(acc)

    def l2n(t):
        return t * lax.rsqrt(jnp.sum(t * t, axis=-1, keepdims=True) + NORM_EPS)

    q = l2n(conv(q_ref, pq_ref, wq_ref, 3 * hp)) * (GDN_DK ** -0.5)
    k = l2n(conv(k_ref, pk_ref, wk_ref, 3 * hp + 1))
    v = conv(v_ref, pv_ref, wv_ref, 3 * hp + 2)

    ab = ab_ref[0, pl.ds(r0, C), :]
    a_col = jnp.sum(jnp.where(sel_a, ab, 0.0), axis=1, keepdims=True)
    b_col = jnp.sum(jnp.where(sel_b, ab, 0.0), axis=1, keepdims=True)
    g_col = neg_ea * _softplus(a_col + dtb)
    beta = _sigmoid(b_col)

    ri = lax.broadcasted_iota(jnp.int32, (C, C), 0)
    ci = lax.broadcasted_iota(jnp.int32, (C, C), 1)
    eye = ri == ci
    causal = ri >= ci
    g_row = jnp.sum(jnp.where(eye, jnp.broadcast_to(g_col, (C, C)), 0.0), axis=0, keepdims=True)
    gc_col = jnp.sum(jnp.where(causal, jnp.broadcast_to(g_row, (C, C)), 0.0), axis=1, keepdims=True)
    gc_row = jnp.sum(jnp.where(eye, jnp.broadcast_to(gc_col, (C, C)), 0.0), axis=0, keepdims=True)
    decay = jnp.where(causal, jnp.exp(jnp.where(causal, gc_col - gc_row, 0.0)), 0.0)
    g_last = gc_row[:, C - 1:C]
    eg = jnp.exp(gc_col)

    kb = k * beta
    yield
    kqk = _dot_nt(jnp.concatenate([kb, q], axis=0).astype(BF16), k.astype(BF16))
    di = lax.broadcasted_iota(jnp.int32, (GDN_DK, GDN_DK), 0)
    dj = lax.broadcasted_iota(jnp.int32, (GDN_DK, GDN_DK), 1)
    ident = jnp.where(di == dj, 1.0, 0.0).astype(BF16)
    k_tail_t = _dot_nt(ident, (k * jnp.exp(g_last - gc_col)).astype(BF16)).astype(BF16)
    yield
    A = jnp.where(ri > ci, kqk[:C] * decay, 0.0)
    attn = kqk[C:] * decay
    X = jnp.concatenate([v * beta, kb * eg], axis=-1)
    Pb = (-A).astype(BF16)
    span = 1
    while span < C:
        PX = _dot(Pb, X.astype(BF16))
        span *= 2
        if span < C:
            PP = _dot(Pb, Pb)
        yield
        X = X + PX
        if span < C:
            Pb = PP.astype(BF16)
    u = X[:, :GDN_DV]
    w = X[:, GDN_DV:]
    wq = jnp.concatenate([w, q * eg], axis=0).astype(BF16)
    egl = jnp.broadcast_to(jnp.exp(g_last), (SUBLANE, LANE))
    return u, wq, k_tail_t, attn.astype(BF16), egl


def _gdn_apply_chunk(r0, C, hp, vals, s_sc, z_ref, onorm_ref, o_ref):
    u, wq, k_tail_t, attn, egl = vals
    lanes = slice(hp * LANE, (hp + 1) * LANE)
    S = s_sc[hp]
    wqS = _dot(wq, S.astype(BF16))
    yield
    v_new = u - wqS[:C]
    v_new_b = v_new.astype(BF16)
    kv = _dot(k_tail_t, v_new_b)
    av = _dot(attn, v_new_b)
    yield
    s_sc[hp] = S * egl[0:1, 0:1] + kv
    o = wqS[C:] + av
    o = o * lax.rsqrt(jnp.mean(o * o, axis=-1, keepdims=True) + NORM_EPS) * onorm_ref[...]
    o = o * _silu(z_ref[0, pl.ds(r0, C), lanes].astype(F32))
    o_ref[0, pl.ds(r0, C), lanes] = o.astype(o_ref.dtype)


def _gdn_core_kernel(q_ref, k_ref, v_ref, wq_ref, wk_ref, wv_ref, pq_ref, pk_ref, pv_ref, ab_ref, z_ref,
                     alog_ref, dtb_ref, onorm_ref, s0_ref, o_ref, sout_ref,
                     win_sc, s_sc, u_sc, wq_sc, kt_sc, at_sc, egl_sc, *, n_chunks, n_hp):
    C = GDN_CHUNK
    lane = lax.broadcasted_iota(jnp.int32, (1, LANE), 1)
    consts = []
    for hp in range(n_hp):
        head = pl.program_id(1) * n_hp + hp
        sel_a = lane == head
        consts.append((sel_a, lane == head + GDN_HEADS, -jnp.exp(_head_scalar(alog_ref, sel_a)),
                       _head_scalar(dtb_ref, sel_a)))
    refs = (q_ref, k_ref, v_ref, wq_ref, wk_ref, wv_ref, pq_ref, pk_ref, pv_ref, ab_ref, win_sc)
    tail = (s_sc, z_ref, onorm_ref, o_ref)

    def save(hp, vals):
        u_sc[hp], wq_sc[hp], kt_sc[hp], at_sc[hp], egl_sc[hp] = vals

    def load(hp):
        return u_sc[hp], wq_sc[hp], kt_sc[hp], at_sc[hp], egl_sc[hp]

    def row(c):
        r0 = N_META + c * C
        return r0 if isinstance(c, int) else pl.multiple_of(r0, BF16_ROWS)

    heads = range(n_hp)

    def prepare(r0, c_len, first):
        return [_gdn_prepare_chunk(r0, c_len, first, hp, refs, consts) for hp in heads]

    def apply(r0, c_len, vals):
        return [_gdn_apply_chunk(r0, c_len, hp, vals[hp], *tail) for hp in heads]

    s_sc[...] = s0_ref[0]
    _interleave(apply(0, N_META, _interleave(prepare(0, N_META, True))))
    for hp, vals in enumerate(_interleave(prepare(N_META, C, False))):
        save(hp, vals)

    def body(c, carry):
        vals = [load(hp) for hp in heads]
        out = _interleave(apply(row(c), C, vals) + prepare(row(c + 1), C, False))
        for hp in heads:
            save(hp, out[n_hp + hp])
        return carry

    lax.fori_loop(0, n_chunks - 1, body, 0)
    _interleave(apply(row(n_chunks - 1), C, [load(hp) for hp in heads]))
    sout_ref[0] = s_sc[...]


def _gdn_core(qkv, conv_w, conv_prev, ab, z, a_log, dt_bias, out_norm, S0):
    B, L, _ = qkv.shape
    n_chunks = (L - N_META) // GDN_CHUNK
    assert n_chunks >= 1 and N_META + n_chunks * GDN_CHUNK == L
    n_hp = 4
    ng = GDN_HEADS // n_hp
    wd = n_hp * LANE
    C = GDN_CHUNK
    col = lambda off: pl.BlockSpec((1, L, wd), lambda b, h: (b, 0, off + h))
    wcol = lambda off: pl.BlockSpec((GDN_CONV, wd), lambda b, h: (0, off + h))
    pcol = lambda off: pl.BlockSpec((1, GDN_CONV - 1, wd), lambda b, h: (b, 0, off + h))
    state = pl.BlockSpec((1, n_hp, GDN_DK, GDN_DV), lambda b, h: (b, h, 0, 0))
    return pl.pallas_call(
        functools.partial(_gdn_core_kernel, n_chunks=n_chunks, n_hp=n_hp),
        grid=(B, ng),
        in_specs=[col(0), col(ng), col(2 * ng), wcol(0), wcol(ng), wcol(2 * ng), pcol(0), pcol(ng), pcol(2 * ng),
                  pl.BlockSpec((1, L, LANE), lambda b, h: (b, 0, 0)), col(0),
                  _full((1, LANE)), _full((1, LANE)), _full((1, GDN_DV)), state],
        out_specs=[col(0), state],
        out_shape=[jax.ShapeDtypeStruct((B, L, GDN_VAL), BF16),
                   jax.ShapeDtypeStruct((B, GDN_HEADS, GDN_DK, GDN_DV), F32)],
        scratch_shapes=[pltpu.VMEM((3 * n_hp, BF16_ROWS + N_META, LANE), F32),
                        pltpu.VMEM((n_hp, GDN_DK, GDN_DV), F32),
                        pltpu.VMEM((n_hp, C, GDN_DV), F32), pltpu.VMEM((n_hp, 2 * C, GDN_DK), BF16),
                        pltpu.VMEM((n_hp, GDN_DK, C), BF16), pltpu.VMEM((n_hp, C, C), BF16),
                        pltpu.VMEM((n_hp, SUBLANE, LANE), F32)],
        compiler_params=_cparams(("parallel", "parallel")),
        name="gdn_core",
    )(qkv, qkv, qkv, conv_w, conv_w, conv_w, conv_prev, conv_prev, conv_prev, ab, z, a_log, dt_bias, out_norm, S0)


def _gdn_step_kernel(new_ref, prev_ref, w_ref, ab_ref, z_ref, alog_ref, dtb_ref, onorm_ref, s_ref, o_ref, sout_ref):
    w = w_ref[...]
    prev = prev_ref[0]
    y = new_ref[0] * w[GDN_CONV - 1:GDN_CONV]
    for j in range(GDN_CONV - 1):
        y = y + prev[j:j + 1] * w[j:j + 1]
    y = _silu(y)
    ab = ab_ref[0]
    z = z_ref[0]
    alog = alog_ref[...]
    dtb = dtb_ref[...]
    ri = lax.broadcasted_iota(jnp.int32, (GDN_DK, GDN_DK), 0)
    ci = lax.broadcasted_iota(jnp.int32, (GDN_DK, GDN_DK), 1)
    eye = ri == ci
    row8 = lax.broadcasted_iota(jnp.int32, (SUBLANE, GDN_DK), 0)

    def l2n(t):
        return t * lax.rsqrt(jnp.sum(t * t, axis=-1, keepdims=True) + NORM_EPS)

    for h in range(GDN_HEADS):
        q = l2n(y[:, h * GDN_DK:(h + 1) * GDN_DK]) * (GDN_DK ** -0.5)
        k = l2n(y[:, GDN_KEY + h * GDN_DK:GDN_KEY + (h + 1) * GDN_DK])
        v = y[:, 2 * GDN_KEY + h * GDN_DV:2 * GDN_KEY + (h + 1) * GDN_DV]
        g = -jnp.exp(alog[:, h:h + 1]) * _softplus(ab[:, h:h + 1] + dtb[:, h:h + 1])
        beta = _sigmoid(ab[:, GDN_HEADS + h:GDN_HEADS + h + 1])
        eg = jnp.exp(g)
        S = s_ref[0, h]
        kq = jnp.where(row8 == 0, jnp.broadcast_to(k, (SUBLANE, GDN_DK)),
                       jnp.where(row8 == 1, jnp.broadcast_to(q, (SUBLANE, GDN_DK)), 0.0))
        kqS = _dot_hi(kq, S)
        v_new = beta * (v - eg * kqS[0:1])
        qk = jnp.sum(q * k, axis=-1, keepdims=True)
        o = eg * kqS[1:2] + qk * v_new
        k_col = jnp.sum(jnp.where(eye, jnp.broadcast_to(k, (GDN_DK, GDN_DK)), 0.0), axis=1, keepdims=True)
        sout_ref[0, h] = S * eg + k_col * v_new
        o = o * lax.rsqrt(jnp.mean(o * o, axis=-1, keepdims=True) + NORM_EPS) * onorm_ref[...]
        o = o * _silu(z[:, h * GDN_DV:(h + 1) * GDN_DV])
        o_ref[0, :, h * GDN_DV:(h + 1) * GDN_DV] = o.astype(o_ref.dtype)


def _gdn_step(qkv_new, conv_prev, conv_w, ab, z, a_log, dt_bias, out_norm, S0):
    Bs = qkv_new.shape[0]
    per = lambda *s: pl.BlockSpec((1,) + s, lambda b: (b,) + (0,) * len(s))
    return pl.pallas_call(
        _gdn_step_kernel,
        grid=(Bs,),
        in_specs=[per(1, GDN_QKV), per(GDN_CONV - 1, GDN_QKV), _full((GDN_CONV, GDN_QKV)), per(1, LANE),
                  per(1, GDN_VAL), _full((1, LANE)), _full((1, LANE)), _full((1, GDN_DV)),
                  per(GDN_HEADS, GDN_DK, GDN_DV)],
        out_specs=[per(1, GDN_VAL), per(GDN_HEADS, GDN_DK, GDN_DV)],
        out_shape=[jax.ShapeDtypeStruct((Bs, 1, GDN_VAL), BF16),
                   jax.ShapeDtypeStruct((Bs, GDN_HEADS, GDN_DK, GDN_DV), F32)],
        compiler_params=_cparams(("parallel",)),
        name="gdn_step",
    )(qkv_new, conv_prev, conv_w, ab, z, a_log, dt_bias, out_norm, S0)


def _proj_residual_kernel(x_ref, o_ref, w_ref, g_ref, out_ref):
    y = _dot(o_ref[0], w_ref[...])
    out_ref[0] = x_ref[0] + _rms(y, g_ref[...])


def _proj_residual(x, o, w, g):
    B, L, D = x.shape
    K = o.shape[-1]
    tl = _row_tile(L, 700, BF16_ROWS)
    return pl.pallas_call(
        _proj_residual_kernel,
        grid=(B, L // tl),
        in_specs=[pl.BlockSpec((1, tl, D), lambda b, l: (b, l, 0)), pl.BlockSpec((1, tl, K), lambda b, l: (b, l, 0)),
                  _full((K, D)), _full((1, D))],
        out_specs=pl.BlockSpec((1, tl, D), lambda b, l: (b, l, 0)),
        out_shape=jax.ShapeDtypeStruct((B, L, D), F32),
        compiler_params=_cparams(("parallel", "parallel")),
        name="proj_residual",
    )(x, o, w, g)


def _ffn_act(j, u, tap1, tap0, cw_ref, cb_ref):
    cw = cw_ref[j]
    conv = u * cw[2:3] + tap1 * cw[1:2] + tap0 * cw[0:1] + cb_ref[j]
    return (_silu(conv[:, :FFN_CHUNK]) * conv[:, FFN_CHUNK:]).astype(BF16)


def _ffn_down(x, a_sc, wdn_ref, gpost_ref):
    a = jnp.concatenate([a_sc[j] for j in range(FFN_NCH)], axis=1)
    return x + _rms(_dot(a, wdn_ref[...]), gpost_ref[...])


def _ffn_seq_kernel(x_ref, gpre_ref, wup_ref, cw_ref, cb_ref, wdn_ref, gpost_ref, prev_ref, out_ref, newprev_ref,
                    h_sc, a_sc, u_sc, carry_sc, *, tl):
    x = x_ref[0]
    h_sc[...] = _rms(x, gpre_ref[...]).astype(BF16)

    @pl.when(pl.program_id(1) == 0)
    def _():
        carry_sc[...] = jnp.zeros_like(carry_sc)
        carry_sc[:, SUBLANE - (FFN_CONV - 1):SUBLANE, :] = prev_ref[0]

    def body(j, carry):
        u = _dot(h_sc[...], wup_ref[j])
        u_sc[0:SUBLANE, :] = carry_sc[j]
        u_sc[SUBLANE:SUBLANE + tl, :] = u
        carry_sc[j] = u[tl - SUBLANE:tl]
        newprev_ref[0, j] = u[tl - (FFN_CONV - 1):tl]
        a_sc[j] = _ffn_act(j, u, u_sc[SUBLANE - 1:SUBLANE - 1 + tl, :], u_sc[SUBLANE - 2:SUBLANE - 2 + tl, :],
                           cw_ref, cb_ref)
        return carry

    lax.fori_loop(0, FFN_NCH, body, 0)
    out_ref[0] = _ffn_down(x, a_sc, wdn_ref, gpost_ref)


def _ffn_step_kernel(x_ref, gpre_ref, wup_ref, cw_ref, cb_ref, wdn_ref, gpost_ref, prev_ref, out_ref, u_ref,
                     h_sc, a_sc):
    x = x_ref[0]
    h_sc[...] = _rms(x, gpre_ref[...]).astype(BF16)

    def body(j, carry):
        u = _dot(h_sc[...], wup_ref[j])
        u_ref[j] = u
        a_sc[j] = _ffn_act(j, u, prev_ref[j, 1], prev_ref[j, 0], cw_ref, cb_ref)
        return carry

    lax.fori_loop(0, FFN_NCH, body, 0)
    out_ref[0] = _ffn_down(x, a_sc, wdn_ref, gpost_ref)


def _ffn_weight_specs():
    w2 = 2 * FFN_CHUNK
    once = pl.Buffered(1)
    return [_full((1, D_MODEL)), _full((FFN_NCH, D_MODEL, w2), once), _full((FFN_NCH, FFN_CONV, w2)),
            _full((FFN_NCH, 1, w2)), _full((D_FF, D_MODEL), once), _full((1, D_MODEL))]


def _chunk_cols(t):
    lead = t.shape[:-1]
    t = t.reshape(lead + (2, FFN_NCH, FFN_CHUNK))
    t = jnp.moveaxis(t, -2, 0)
    return t.reshape((FFN_NCH,) + lead + (2 * FFN_CHUNK,))


def _unchunk_cols(t):
    lead = t.shape[1:-1]
    t = t.reshape((FFN_NCH,) + lead + (2, FFN_CHUNK))
    t = jnp.moveaxis(t, 0, -2)
    return t.reshape(lead + (2 * D_FF,))


def _ffn_seq(x, prev, wts):
    B, L, D = x.shape
    tl = _row_tile(L, 700, BF16_ROWS)
    w2 = 2 * FFN_CHUNK
    prev_c = jnp.moveaxis(_chunk_cols(prev), 0, 1)
    out, newprev = pl.pallas_call(
        functools.partial(_ffn_seq_kernel, tl=tl),
        grid=(B, L // tl),
        in_specs=[pl.BlockSpec((1, tl, D), lambda b, l: (b, l, 0))] + _ffn_weight_specs()
        + [pl.BlockSpec((1, FFN_NCH, FFN_CONV - 1, w2), lambda b, l: (b, 0, 0, 0))],
        out_specs=[pl.BlockSpec((1, tl, D), lambda b, l: (b, l, 0)),
                   pl.BlockSpec((1, FFN_NCH, FFN_CONV - 1, w2), lambda b, l: (b, 0, 0, 0))],
        out_shape=[jax.ShapeDtypeStruct((B, L, D), F32), jax.ShapeDtypeStruct((B, FFN_NCH, FFN_CONV - 1, w2), F32)],
        scratch_shapes=[pltpu.VMEM((tl, D), BF16), pltpu.VMEM((FFN_NCH, tl, FFN_CHUNK), BF16),
                        pltpu.VMEM((SUBLANE + tl, w2), F32),
                        pltpu.VMEM((FFN_NCH, SUBLANE, w2), F32)],
        compiler_params=_cparams(("parallel", "arbitrary")),
        name="ffn_seq",
    )(x, *wts, prev_c)
    return out, _unchunk_cols(jnp.moveaxis(newprev, 1, 0))


def _ffn_step(x, prev, wts):
    _, Bs, D = x.shape
    w2 = 2 * FFN_CHUNK
    prev_c = _chunk_cols(jnp.moveaxis(prev, 1, 0))
    out, u = pl.pallas_call(
        _ffn_step_kernel,
        grid=(1,),
        in_specs=[_full((1, Bs, D))] + _ffn_weight_specs() + [_full((FFN_NCH, FFN_CONV - 1, Bs, w2))],
        out_specs=[_full((1, Bs, D)), _full((FFN_NCH, Bs, w2))],
        out_shape=[jax.ShapeDtypeStruct((1, Bs, D), F32), jax.ShapeDtypeStruct((FFN_NCH, Bs, w2), F32)],
        scratch_shapes=[pltpu.VMEM((Bs, D), BF16), pltpu.VMEM((FFN_NCH, Bs, FFN_CHUNK), BF16)],
        compiler_params=_cparams(("arbitrary",)),
        name="ffn_step",
    )(x, *wts, prev_c)
    return out, jnp.concatenate([prev[:, 1:], _unchunk_cols(u)[:, None]], axis=1)


def _rot_half(t, width):
    lane = lax.broadcasted_iota(jnp.int32, t.shape, t.ndim - 1)
    first = (lane % MLA_ROPE) < (MLA_ROPE // 2)
    return jnp.where(first, pltpu.roll(t, width - MLA_ROPE // 2, t.ndim - 1), pltpu.roll(t, MLA_ROPE // 2, t.ndim - 1))


def _mla_proj_kernel(x_ref, gkv_ref, wkva_ref, gkva_ref, gq_ref, wqa_ref, gqa_ref, wqb_ref, wuk_ref, cs_ref,
                     c_ref, kr_ref, kcat_ref, q_ref):
    x = x_ref[0]
    cos = cs_ref[:, 0:LANE]
    sin = cs_ref[:, LANE:2 * LANE]
    lane = lax.broadcasted_iota(jnp.int32, (1, LANE), 1)

    ckv = _dot(_rms(x, gkv_ref[...]).astype(BF16), wkva_ref[...])
    c = _rms(ckv[:, :MLA_KV_LORA], gkva_ref[...])
    r = ckv[:, MLA_KV_LORA:MLA_QK]
    kr = r * cos + _rot_half(r, LANE) * sin
    c_ref[0] = c
    kr_ref[0] = kr[:, :MLA_ROPE]
    kcat_ref[0, :, 0:MLA_KV_LORA] = c.astype(BF16)
    kcat_ref[0, :, MLA_KV_LORA:MLA_QK] = kr.astype(BF16)

    qa = _dot(_rms(x, gq_ref[...]).astype(BF16), wqa_ref[...])
    qb = _dot(_rms(qa, gqa_ref[...]).astype(BF16), wqb_ref[...])
    n_nope = MLA_HEADS * MLA_NOPE
    n_pe = MLA_HEADS * MLA_ROPE
    pe = qb[:, n_nope:n_nope + n_pe]
    reps = n_pe // LANE
    pe = pe * jnp.concatenate([cos] * reps, axis=1) + _rot_half(pe, n_pe) * jnp.concatenate([sin] * reps, axis=1)
    for h in range(MLA_HEADS):
        q_lat = _dot(qb[:, h * MLA_NOPE:(h + 1) * MLA_NOPE].astype(BF16), wuk_ref[h])
        q_ref[0, h, :, 0:MLA_KV_LORA] = (q_lat * MLA_SCALE).astype(BF16)
        t = pe[:, (h // 2) * LANE:(h // 2 + 1) * LANE]
        if h % 2 == 1:
            t = pltpu.roll(t, MLA_ROPE, 1)
        q_ref[0, h, :, MLA_KV_LORA:MLA_QK] = jnp.where(lane < MLA_ROPE, t * MLA_SCALE, 0.0).astype(BF16)


def _mla_proj(x, cs, gkv, wkva, gkva, gq, wqa, gqa, wqb, wuk):
    B, L, D = x.shape
    tl = _row_tile(L, 700, BF16_ROWS)
    row = lambda n: pl.BlockSpec((1, tl, n), lambda b, l: (b, l, 0))
    return pl.pallas_call(
        _mla_proj_kernel,
        grid=(B, L // tl),
        in_specs=[row(D), _full((1, D)), _full((D, MLA_QK)), _full((1, MLA_KV_LORA)), _full((1, D)),
                  _full((D, MLA_Q_LORA)), _full((1, MLA_Q_LORA)), _full(wqb.shape), _full(wuk.shape),
                  pl.BlockSpec((tl, 2 * LANE), lambda b, l: (l, 0))],
        out_specs=[row(MLA_KV_LORA), row(MLA_ROPE), row(MLA_QK),
                   pl.BlockSpec((1, MLA_HEADS, tl, MLA_QK), lambda b, l: (b, 0, l, 0))],
        out_shape=[jax.ShapeDtypeStruct((B, L, MLA_KV_LORA), F32), jax.ShapeDtypeStruct((B, L, MLA_ROPE), F32),
                   jax.ShapeDtypeStruct((B, L, MLA_QK), BF16),
                   jax.ShapeDtypeStruct((B, MLA_HEADS, L, MLA_QK), BF16)],
        compiler_params=_cparams(("parallel", "parallel")),
        name="mla_proj",
    )(x, gkv, wkva, gkva, gq, wqa, gqa, wqb, wuk, cs)


def _attn_kernel(q_ref, kcat_ref, o_ref, m_sc, l_sc, acc_sc, *, tq):
    i = pl.program_id(1)
    q = q_ref[0, 0]
    m_sc[...] = jnp.full_like(m_sc, -jnp.inf)
    l_sc[...] = jnp.zeros_like(l_sc)
    acc_sc[...] = jnp.zeros_like(acc_sc)
    ri = lax.broadcasted_iota(jnp.int32, (tq, tq), 0)
    ci = lax.broadcasted_iota(jnp.int32, (tq, tq), 1)

    def tile(kt, masked):
        kv = kcat_ref[0, pl.ds(pl.multiple_of(kt * tq, BF16_ROWS), tq), :]
        s = _dot_nt(q, kv)
        if masked:
            s = jnp.where(ci <= ri, s, -jnp.inf)
        m_prev = m_sc[...]
        m_new = jnp.maximum(m_prev, jnp.max(s, axis=1, keepdims=True))
        alpha = jnp.exp(m_prev - m_new)
        p = jnp.exp(s - m_new)
        l_sc[...] = alpha * l_sc[...] + jnp.sum(p, axis=1, keepdims=True)
        acc_sc[...] = alpha * acc_sc[...] + _dot(p.astype(BF16), kv[:, :MLA_KV_LORA])
        m_sc[...] = m_new

    def body(kt, carry):
        tile(kt, False)
        return carry

    lax.fori_loop(0, i, body, 0)
    tile(i, True)
    o_ref[0, 0] = (acc_sc[...] / l_sc[...]).astype(o_ref.dtype)


def _attn_seq(q, kcat):
    B, nh, L, _ = q.shape
    tq = _row_tile(L, 700, BF16_ROWS)
    return pl.pallas_call(
        functools.partial(_attn_kernel, tq=tq),
        grid=(B, L // tq, nh),
        in_specs=[pl.BlockSpec((1, 1, tq, MLA_QK), lambda b, i, h: (b, h, i, 0)),
                  pl.BlockSpec((1, L, MLA_QK), lambda b, i, h: (b, 0, 0))],
        out_specs=pl.BlockSpec((1, 1, tq, MLA_KV_LORA), lambda b, i, h: (b, h, i, 0)),
        out_shape=jax.ShapeDtypeStruct((B, nh, L, MLA_KV_LORA), BF16),
        scratch_shapes=[pltpu.VMEM((tq, 1), F32), pltpu.VMEM((tq, 1), F32), pltpu.VMEM((tq, MLA_KV_LORA), F32)],
        compiler_params=_cparams(("parallel", "parallel", "parallel")),
        name="attn_seq",
    )(q, kcat)


def _attn_paged_kernel(pt_ref, q_ref, kcat_ref, cache_c_ref, cache_k_ref, o_ref, cbuf, kbuf, c16, s_sc, sem,
                       *, n_pages, page, kchunk):
    b = pl.program_id(0)
    nb = pl.num_programs(0)
    P = n_pages * page

    def copies(bb, slot):
        out = []
        for pg in range(n_pages):
            idx = pt_ref[bb, pg]
            out.append(pltpu.make_async_copy(cache_c_ref.at[idx], cbuf.at[slot, pl.ds(pg * page, page), :],
                                             sem.at[0, slot]))
            out.append(pltpu.make_async_copy(cache_k_ref.at[idx], kbuf.at[slot, :, pl.ds(pg * page, page)],
                                             sem.at[1, slot]))
        return out

    @pl.when(b == 0)
    def _():
        for cp in copies(0, 0):
            cp.start()

    @pl.when(b + 1 < nb)
    def _():
        for cp in copies(b + 1, (b + 1) % 2):
            cp.start()

    slot = b % 2
    for cp in copies(b, slot):
        cp.wait()

    q = q_ref[0]
    q_lat = q[:, :MLA_KV_LORA]
    q_pe = q[:, MLA_KV_LORA:MLA_KV_LORA + MLA_ROPE]
    for c0 in range(0, P, kchunk):
        cc = cbuf[slot, c0:c0 + kchunk, :].astype(BF16)
        c16[c0:c0 + kchunk, :] = cc
        kk = kbuf[slot, :, c0:c0 + kchunk].astype(BF16)
        s_sc[:, c0:c0 + kchunk] = _dot_nt(q_lat, cc) + _dot(q_pe, kk)
    kself = kcat_ref[0]
    s_self = jnp.sum(q.astype(F32) * kself.astype(F32), axis=-1, keepdims=True)
    s = s_sc[...]
    m = jnp.maximum(jnp.max(s, axis=-1, keepdims=True), s_self)
    p = jnp.exp(s - m)
    p_self = jnp.exp(s_self - m)
    denom = jnp.sum(p, axis=-1, keepdims=True) + p_self
    s_sc[...] = p
    acc = p_self * kself[:, :MLA_KV_LORA].astype(F32)
    for c0 in range(0, P, kchunk):
        acc = acc + _dot(s_sc[:, c0:c0 + kchunk].astype(BF16), c16[c0:c0 + kchunk, :])
    o_ref[0] = (acc / denom).astype(o_ref.dtype)


def _attn_paged(q, kcat, cache_c, cache_k, page_table):
    Bs, nh, _ = q.shape
    n_pages = page_table.shape[1]
    page = cache_c.shape[1]
    cache_k = jnp.swapaxes(cache_k, 1, 2)
    P = n_pages * page
    kchunk = _row_tile(P, 1024, LANE)
    grid_spec = pltpu.PrefetchScalarGridSpec(
        num_scalar_prefetch=1,
        grid=(Bs,),
        in_specs=[pl.BlockSpec((1, nh, MLA_QK), lambda b, pt: (b, 0, 0)),
                  pl.BlockSpec((1, 1, MLA_QK), lambda b, pt: (b, 0, 0)),
                  pl.BlockSpec(memory_space=pl.ANY), pl.BlockSpec(memory_space=pl.ANY)],
        out_specs=pl.BlockSpec((1, nh, MLA_KV_LORA), lambda b, pt: (b, 0, 0)),
        scratch_shapes=[pltpu.VMEM((2, P, MLA_KV_LORA), F32), pltpu.VMEM((2, MLA_ROPE, P), F32),
                        pltpu.VMEM((P, MLA_KV_LORA), BF16), pltpu.VMEM((nh, P), F32),
                        pltpu.SemaphoreType.DMA((2, 2))],
    )
    return pl.pallas_call(
        functools.partial(_attn_paged_kernel, n_pages=n_pages, page=page, kchunk=kchunk),
        grid_spec=grid_spec,
        out_shape=jax.ShapeDtypeStruct((Bs, nh, MLA_KV_LORA), BF16),
        compiler_params=_cparams(("arbitrary",)),
        name="attn_paged",
    )(page_table, q, kcat, cache_c, cache_k)


def _mla_out_kernel(x_ref, o_ref, wuv_ref, wout_ref, g_ref, out_ref, acc_sc):
    for h in range(MLA_HEADS):
        oh = _dot(o_ref[0, h], wuv_ref[h]).astype(BF16)
        y = _dot(oh, wout_ref[h])
        if h == 0:
            acc_sc[...] = y
        else:
            acc_sc[...] += y
    out_ref[0] = x_ref[0] + _rms(acc_sc[...], g_ref[...])


def _mla_out(x, o_lat, wuv, wout, g):
    B, L, D = x.shape
    tl = _row_tile(L, 700, BF16_ROWS)
    return pl.pallas_call(
        _mla_out_kernel,
        grid=(B, L // tl),
        in_specs=[pl.BlockSpec((1, tl, D), lambda b, l: (b, l, 0)),
                  pl.BlockSpec((1, MLA_HEADS, tl, MLA_KV_LORA), lambda b, l: (b, 0, l, 0)),
                  _full(wuv.shape), _full(wout.shape), _full((1, D))],
        out_specs=pl.BlockSpec((1, tl, D), lambda b, l: (b, l, 0)),
        out_shape=jax.ShapeDtypeStruct((B, L, D), F32),
        scratch_shapes=[pltpu.VMEM((tl, D), F32)],
        compiler_params=_cparams(("parallel", "parallel")),
        name="mla_out",
    )(x, o_lat, wuv, wout, g)


def _rope_table(positions):
    half = MLA_ROPE // 2
    inv = ROPE_THETA ** (-jnp.arange(half, dtype=F32) / half)
    ang = positions.astype(F32)[:, None] * inv[None, :]
    cos, sin = jnp.cos(ang), jnp.sin(ang)
    reps = LANE // MLA_ROPE
    return jnp.concatenate([jnp.concatenate([cos, cos] * reps, axis=1), jnp.concatenate([-sin, sin] * reps, axis=1)],
                           axis=1)


def _row(v, width=None):
    v = v.reshape(1, -1).astype(F32)
    if width is not None and v.shape[1] < width:
        v = jnp.pad(v, ((0, 0), (0, width - v.shape[1])))
    return v


def _prepare(p):
    i = 0
    w_in = p['a_w_in'][i]
    n_ab = 2 * GDN_HEADS
    wq_b = p['b_w_q_b'][i].reshape(MLA_Q_LORA, MLA_HEADS, MLA_NOPE + MLA_ROPE)
    wq_b = jnp.concatenate([wq_b[:, :, :MLA_NOPE].reshape(MLA_Q_LORA, -1), wq_b[:, :, MLA_NOPE:].reshape(MLA_Q_LORA, -1)],
                           axis=1)
    prep = {
        'a_norm_pre': _row(p['a_norm_pre'][i]), 'a_norm_post': _row(p['a_norm_post'][i]),
        'a_wqkv': w_in[:, :GDN_QKV].astype(BF16), 'a_wz': w_in[:, GDN_QKV:GDN_QKV + GDN_VAL].astype(BF16),
        'a_wab': jnp.pad(w_in[:, GDN_QKV + GDN_VAL:], ((0, 0), (0, LANE - n_ab))).astype(BF16),
        'a_conv_w': p['a_conv_w'][i].astype(F32), 'a_log': _row(p['a_log'][i], LANE),
        'a_dt_bias': _row(p['a_dt_bias'][i], LANE), 'a_out_norm': _row(p['a_out_norm'][i]),
        'a_w_out': p['a_w_out'][i].astype(BF16),
        'kv_norm': _row(p['kv_norm']),
        'kv_w_a': jnp.pad(p['kv_w_a'], ((0, 0), (0, MLA_QK - MLA_KV_LORA - MLA_ROPE))).astype(BF16),
        'kv_a_norm': _row(p['kv_a_norm']),
        'kv_w_uk': jnp.transpose(p['kv_w_uk'], (1, 2, 0)).astype(BF16),
        'kv_w_uv': jnp.transpose(p['kv_w_uv'], (1, 0, 2)).astype(BF16),
        'b_norm_pre': _row(p['b_norm_pre'][i]), 'b_norm_post': _row(p['b_norm_post'][i]),
        'b_w_q_a': p['b_w_q_a'][i].astype(BF16), 'b_q_a_norm': _row(p['b_q_a_norm'][i]),
        'b_w_q_b': wq_b.astype(BF16),
        'b_w_out': p['b_w_out'][i].reshape(MLA_HEADS, MLA_V, D_MODEL).astype(BF16),
        'ffn': [],
    }
    for layer in range(2):
        w_up = jnp.moveaxis(_chunk_cols(p['f_w_up'][layer]), 0, 0).astype(BF16)
        prep['ffn'].append((
            _row(p['f_norm_pre'][layer]), w_up, _chunk_cols(p['f_conv_w'][layer]).astype(F32),
            _chunk_cols(p['f_conv_b'][layer].reshape(1, -1)).astype(F32),
            p['f_w_down'][layer].astype(BF16), _row(p['f_norm_post'][layer])))
    return prep


def _mla_block(x, cs, prep):
    return _mla_proj(x, cs, prep['kv_norm'], prep['kv_w_a'], prep['kv_a_norm'], prep['b_norm_pre'], prep['b_w_q_a'],
                     prep['b_q_a_norm'], prep['b_w_q_b'], prep['kv_w_uk'])


def _trunk_seq(x, S0, dconv0, fconv0, prep):
    B, L, _ = x.shape
    assert L >= SUBLANE
    qkv, z, ab, tail = _gdn_in(x, prep['a_norm_pre'], prep['a_wqkv'], prep['a_wz'], prep['a_wab'], BF16)
    o, S = _gdn_core(qkv, prep['a_conv_w'], dconv0[0], ab, z, prep['a_log'], prep['a_dt_bias'], prep['a_out_norm'], S0[0])
    dconv = tail[:, SUBLANE - (GDN_CONV - 1):]
    x = _proj_residual(x, o, prep['a_w_out'], prep['a_norm_post'])
    x, fconv_a = _ffn_seq(x, fconv0[0], prep['ffn'][0])
    cs = _rope_table(jnp.arange(L, dtype=jnp.int32))
    c, kr, kcat, q = _mla_block(x, cs, prep)
    o_lat = _attn_seq(q, kcat)
    x = _mla_out(x, o_lat, prep['kv_w_uv'], prep['b_w_out'], prep['b_norm_post'])
    x, fconv_b = _ffn_seq(x, fconv0[1], prep['ffn'][1])
    return x, S[None], dconv[None], jnp.stack([fconv_a, fconv_b]), c, kr


def _trunk_step(x, pos, S0, dconv0, fconv0, cache_c, cache_k, page_table, prep):
    Bs = x.shape[0]
    xt = x.reshape(1, Bs, D_MODEL)
    qkv, z, ab, _ = _gdn_in(xt, prep['a_norm_pre'], prep['a_wqkv'], prep['a_wz'], prep['a_wab'], F32)
    o, S = _gdn_step(qkv.reshape(Bs, 1, -1), dconv0[0], prep['a_conv_w'], ab.reshape(Bs, 1, -1), z.reshape(Bs, 1, -1),
                     prep['a_log'], prep['a_dt_bias'], prep['a_out_norm'], S0[0])
    dconv = jnp.concatenate([dconv0[0][:, 1:], qkv.reshape(Bs, 1, -1)], axis=1)
    xt = _proj_residual(xt, o.reshape(1, Bs, -1), prep['a_w_out'], prep['a_norm_post'])
    xt, fconv_a = _ffn_step(xt, fconv0[0], prep['ffn'][0])
    cs = _rope_table(jnp.full((Bs,), pos, jnp.int32))
    c, kr, kcat, q = _mla_block(xt, cs, prep)
    q = jnp.moveaxis(q[0], 0, 1)
    o_lat = _attn_paged(q, kcat.reshape(Bs, 1, MLA_QK), cache_c, cache_k, page_table)
    o_lat = jnp.moveaxis(o_lat, 0, 1)[None]
    xt = _mla_out(xt, o_lat, prep['kv_w_uv'], prep['b_w_out'], prep['b_norm_post'])
    xt, fconv_b = _ffn_step(xt, fconv0[1], prep['ffn'][1])
    return (xt.reshape(Bs, 1, D_MODEL), S[None], dconv[None], jnp.stack([fconv_a, fconv_b]),
            c.reshape(Bs, 1, MLA_KV_LORA), kr.reshape(Bs, 1, MLA_ROPE))


def kernel(x_prompt, x_sample, state_delta_S, state_delta_conv, state_ffn_conv, cache_kv_latent, cache_k_rope,
           page_table, meta_tokens, a_norm_pre, a_norm_post, a_w_in, a_conv_w, a_log, a_dt_bias, a_out_norm, a_w_out,
           kv_norm, kv_w_a, kv_a_norm, kv_w_uk, kv_w_uv, b_norm_pre, b_norm_post, b_w_q_a, b_q_a_norm, b_w_q_b,
           b_w_out, f_norm_pre, f_norm_post, f_w_up, f_conv_w, f_conv_b, f_w_down):
    prep = _prepare({
        'a_norm_pre': a_norm_pre, 'a_norm_post': a_norm_post, 'a_w_in': a_w_in, 'a_conv_w': a_conv_w, 'a_log': a_log,
        'a_dt_bias': a_dt_bias, 'a_out_norm': a_out_norm, 'a_w_out': a_w_out, 'kv_norm': kv_norm, 'kv_w_a': kv_w_a,
        'kv_a_norm': kv_a_norm, 'kv_w_uk': kv_w_uk, 'kv_w_uv': kv_w_uv, 'b_norm_pre': b_norm_pre,
        'b_norm_post': b_norm_post, 'b_w_q_a': b_w_q_a, 'b_q_a_norm': b_q_a_norm, 'b_w_q_b': b_w_q_b,
        'b_w_out': b_w_out, 'f_norm_pre': f_norm_pre, 'f_norm_post': f_norm_post, 'f_w_up': f_w_up,
        'f_conv_w': f_conv_w, 'f_conv_b': f_conv_b, 'f_w_down': f_w_down})

    bp = x_prompt.shape[0]
    xp = jnp.concatenate([jnp.broadcast_to(meta_tokens.astype(x_prompt.dtype)[None], (bp, N_META, D_MODEL)), x_prompt],
                         axis=1)
    zS = jnp.zeros((1, bp, GDN_HEADS, GDN_DK, GDN_DV), state_delta_S.dtype)
    zdc = jnp.zeros((1, bp, GDN_CONV - 1, GDN_QKV), x_prompt.dtype)
    zfc = jnp.zeros((2, bp, FFN_CONV - 1, 2 * D_FF), x_prompt.dtype)
    yp, p_S, p_dconv, p_fconv, p_c, p_kr = _trunk_seq(xp, zS, zdc, zfc, prep)

    past_len = page_table.shape[1] * cache_kv_latent.shape[1]
    ys, s_S, s_dconv, s_fconv, s_c, s_kr = _trunk_step(x_sample, past_len, state_delta_S, state_delta_conv,
                                                       state_ffn_conv, cache_kv_latent, cache_k_rope, page_table, prep)
    return (yp[:, N_META:], ys, p_S, p_dconv, p_fconv, p_c, p_kr, s_S, s_dconv, s_fconv, s_c, s_kr)
```

```python
import functools
import math

import jax
import jax.numpy as jnp
from jax import lax
from jax.experimental import pallas as pl
from jax.experimental.pallas import tpu as pltpu

F32 = jnp.float32
BF16 = jnp.bfloat16

D_MODEL = 1024
N_META = 16
NORM_EPS = 1e-6

GDN_HEADS = 8
GDN_DK = 128
GDN_DV = 128
GDN_CONV = 4
GDN_CHUNK = 64
GDN_KEY = GDN_HEADS * GDN_DK
GDN_VAL = GDN_HEADS * GDN_DV
GDN_QKV = 2 * GDN_KEY + GDN_VAL
GDN_SOLVE_SPLIT = 3

MLA_HEADS = 8
MLA_Q_LORA = 384
MLA_KV_LORA = 256
MLA_NOPE = 128
MLA_ROPE = 64
MLA_V = 128
MLA_SCALE = 1.0 / math.sqrt(MLA_NOPE + MLA_ROPE)
ROPE_THETA = 10000.0
MLA_QK = MLA_KV_LORA + 128

D_FF = 2816
FFN_CONV = 3
FFN_CHUNK = 256
FFN_NCH = D_FF // FFN_CHUNK

LANE = 128
SUBLANE = 8
BF16_ROWS = 16
VMEM_LIMIT = 56 * 1024 * 1024

HI = lax.Precision.HIGHEST


def _row_tile(n, cap, align):
    best = None
    for t in range(align, min(n, cap) + 1, align):
        if n % t == 0:
            best = t
    return n if best is None else best


def _cparams(sem, flags=None):
    return pltpu.CompilerParams(dimension_semantics=sem, vmem_limit_bytes=VMEM_LIMIT, flags=flags)


def _rms(x, w):
    return x * lax.rsqrt(jnp.mean(x * x, axis=-1, keepdims=True) + NORM_EPS) * w


def _sigmoid(x):
    return 1.0 / (1.0 + jnp.exp(-x))


def _silu(x):
    return x * _sigmoid(x)


def _softplus(x):
    return jnp.maximum(x, 0.0) + jnp.log(1.0 + jnp.exp(-jnp.abs(x)))


def _dot(a, b):
    return jnp.dot(a, b, preferred_element_type=F32)


def _dot_nt(a, b, precision=None):
    return lax.dot_general(a, b, (((1,), (1,)), ((), ())), precision=precision, preferred_element_type=F32)


def _split_bf16(t):
    hi = t.astype(BF16)
    return hi, (t - hi.astype(F32)).astype(BF16)


def _dot_parts(a_parts, b_parts):
    out = _dot(a_parts[0], b_parts[0])
    if len(a_parts) > 1:
        out = out + (_dot(a_parts[0], b_parts[1]) + _dot(a_parts[1], b_parts[0]))
    return out


def _interleave(gens):
    results = [None] * len(gens)
    active = list(range(len(gens)))
    while active:
        for i in list(active):
            try:
                next(gens[i])
            except StopIteration as stop:
                results[i] = stop.value
                active.remove(i)
    return results


def _dot_hi(a, b):
    return jnp.dot(a, b, precision=HI, preferred_element_type=F32)


def _full(shape, pipeline_mode=None):
    nd = len(shape)
    return pl.BlockSpec(shape, lambda *_: (0,) * nd, pipeline_mode=pipeline_mode)


def _gdn_in_kernel(x_ref, g_ref, wqkv_ref, wz_ref, wab_ref, qkv_ref, z_ref, ab_ref, tail_ref, *, tl):
    h = _rms(x_ref[0], g_ref[...]).astype(BF16)
    for n0 in range(0, GDN_QKV, 512):
        r = _dot(h, wqkv_ref[:, n0:n0 + 512])
        qkv_ref[0, :, n0:n0 + 512] = r.astype(qkv_ref.dtype)
        tail_ref[0, :, n0:n0 + 512] = r[tl - SUBLANE:tl]
    for n0 in range(0, GDN_VAL, 512):
        z_ref[0, :, n0:n0 + 512] = _dot(h, wz_ref[:, n0:n0 + 512]).astype(z_ref.dtype)
    ab_ref[0] = _dot(h, wab_ref[...])


def _gdn_in(x, g, wqkv, wz, wab, act_dtype):
    B, L, D = x.shape
    tl = _row_tile(L, 700, BF16_ROWS)
    row = lambda n: pl.BlockSpec((1, tl, n), lambda b, l: (b, l, 0))
    return pl.pallas_call(
        functools.partial(_gdn_in_kernel, tl=tl),
        grid=(B, L // tl),
        in_specs=[row(D), _full((1, D)), _full((D, GDN_QKV)), _full((D, GDN_VAL)), _full((D, LANE))],
        out_specs=[row(GDN_QKV), row(GDN_VAL), row(LANE), pl.BlockSpec((1, SUBLANE, GDN_QKV), lambda b, l: (b, 0, 0))],
        out_shape=[jax.ShapeDtypeStruct((B, L, GDN_QKV), act_dtype), jax.ShapeDtypeStruct((B, L, GDN_VAL), act_dtype),
                   jax.ShapeDtypeStruct((B, L, LANE), F32), jax.ShapeDtypeStruct((B, SUBLANE, GDN_QKV), F32)],
        compiler_params=_cparams(("parallel", "arbitrary")),
        name="gdn_in",
    )(x, g, wqkv, wz, wab)


def _head_scalar(ref, sel):
    return jnp.sum(jnp.where(sel, ref[...], 0.0), axis=1, keepdims=True)


def _gdn_prepare_chunk(r0, C, first, hp, refs, consts):
    (q_ref, k_ref, v_ref, wq_ref, wk_ref, wv_ref, pq_ref, pk_ref, pv_ref, ab_ref, win_sc) = refs
    sel_a, sel_b, neg_ea, dtb = consts[hp]
    lanes = slice(hp * LANE, (hp + 1) * LANE)
    halo = BF16_ROWS

    def window(src_ref, prev_ref, slot):
        if first:
            win_sc[slot, 0:halo, :] = jnp.zeros((halo, LANE), F32)
            win_sc[slot, halo - (GDN_CONV - 1):halo, :] = prev_ref[0, :, lanes]
            win_sc[slot, halo:halo + C, :] = src_ref[0, 0:C, lanes].astype(F32)
            return win_sc[slot, 0:halo + C, :]
        return src_ref[0, pl.ds(r0 - halo, C + halo), lanes].astype(F32)

    def conv(src_ref, prev_ref, w_ref, slot):
        win = window(src_ref, prev_ref, slot)
        w = w_ref[:, lanes]
        acc = win[halo:halo + C] * w[GDN_CONV - 1:GDN_CONV]
        for j in range(GDN_CONV - 1):
            off = halo - (GDN_CONV - 1) + j
            acc = acc + win[off:off + C] * w[j:j + 1]
        return _silu(acc)

    def l2n(t):
        return t * lax.rsqrt(jnp.sum(t * t, axis=-1, keepdims=True) + NORM_EPS)

    q = l2n(conv(q_ref, pq_ref, wq_ref, 3 * hp)) * (GDN_DK ** -0.5)
    k = l2n(conv(k_ref, pk_ref, wk_ref, 3 * hp + 1))
    v = conv(v_ref, pv_ref, wv_ref, 3 * hp + 2)

    ab = ab_ref[0, pl.ds(r0, C), :]
    a_col = jnp.sum(jnp.where(sel_a, ab, 0.0), axis=1, keepdims=True)
    b_col = jnp.sum(jnp.where(sel_b, ab, 0.0), axis=1, keepdims=True)
    g_col = neg_ea * _softplus(a_col + dtb)
    beta = _sigmoid(b_col)

    ri = lax.broadcasted_iota(jnp.int32, (C, C), 0)
    ci = lax.broadcasted_iota(jnp.int32, (C, C), 1)
    eye = ri == ci
    causal = ri >= ci
    g_row = jnp.sum(jnp.where(eye, jnp.broadcast_to(g_col, (C, C)), 0.0), axis=0, keepdims=True)
    gc_col = jnp.sum(jnp.where(causal, jnp.broadcast_to(g_row, (C, C)), 0.0), axis=1, keepdims=True)
    gc_row = jnp.sum(jnp.where(eye, jnp.broadcast_to(gc_col, (C, C)), 0.0), axis=0, keepdims=True)
    decay = jnp.where(causal, jnp.exp(jnp.where(causal, gc_col - gc_row, 0.0)), 0.0)
    g_last = gc_row[:, C - 1:C]
    eg = jnp.exp(gc_col)

    kb = k * beta
    yield
    kqk = _dot_nt(jnp.concatenate([kb, q], axis=0).astype(BF16), k.astype(BF16))
    di = lax.broadcasted_iota(jnp.int32, (GDN_DK, GDN_DK), 0)
    dj = lax.broadcasted_iota(jnp.int32, (GDN_DK, GDN_DK), 1)
    ident = jnp.where(di == dj, 1.0, 0.0).astype(BF16)
    k_tail_t = _dot_nt(ident, (k * jnp.exp(g_last - gc_col)).astype(BF16)).astype(BF16)
    yield
    A = jnp.where(ri > ci, kqk[:C] * decay, 0.0)
    attn = kqk[C:] * decay
    X = jnp.concatenate([v * beta, kb * eg], axis=-1)
    P = -A
    span = 1
    level = 0
    while span < C:
        split = level < GDN_SOLVE_SPLIT
        p_parts = _split_bf16(P) if split else (P.astype(BF16),)
        x_parts = _split_bf16(X) if split else (X.astype(BF16),)
        PX = _dot_parts(p_parts, x_parts)
        span *= 2
        level += 1
        if span < C:
            PP = _dot_parts(p_parts, p_parts)
        yield
        X = X + PX
        if span < C:
            P = PP
    u = X[:, :GDN_DV]
    w = X[:, GDN_DV:]
    wq = jnp.concatenate([w, q * eg], axis=0).astype(BF16)
    egl = jnp.broadcast_to(jnp.exp(g_last), (SUBLANE, LANE))
    return u, wq, k_tail_t, attn.astype(BF16), egl


def _gdn_apply_chunk(r0, C, hp, vals, s_sc, z_ref, onorm_ref, o_ref):
    u, wq, k_tail_t, attn, egl = vals
    lanes = slice(hp * LANE, (hp + 1) * LANE)
    S = s_sc[hp]
    wqS = _dot(wq, S.astype(BF16))
    yield
    v_new = u - wqS[:C]
    v_new_b = v_new.astype(BF16)
    kv = _dot(k_tail_t, v_new_b)
    av = _dot(attn, v_new_b)
    yield
    s_sc[hp] = S * egl[0:1, 0:1] + kv
    o = wqS[C:] + av
    o = o * lax.rsqrt(jnp.mean(o * o, axis=-1, keepdims=True) + NORM_EPS) * onorm_ref[...]
    o = o * _silu(z_ref[0, pl.ds(r0, C), lanes].astype(F32))
    o_ref[0, pl.ds(r0, C), lanes] = o.astype(o_ref.dtype)


def _gdn_core_kernel(q_ref, k_ref, v_ref, wq_ref, wk_ref, wv_ref, pq_ref, pk_ref, pv_ref, ab_ref, z_ref,
                     alog_ref, dtb_ref, onorm_ref, s0_ref, o_ref, sout_ref,
                     win_sc, s_sc, u_sc, wq_sc, kt_sc, at_sc, egl_sc, *, n_chunks, n_hp):
    C = GDN_CHUNK
    lane = lax.broadcasted_iota(jnp.int32, (1, LANE), 1)
    consts = []
    for hp in range(n_hp):
        head = pl.program_id(1) * n_hp + hp
        sel_a = lane == head
        consts.append((sel_a, lane == head + GDN_HEADS, -jnp.exp(_head_scalar(alog_ref, sel_a)),
                       _head_scalar(dtb_ref, sel_a)))
    refs = (q_ref, k_ref, v_ref, wq_ref, wk_ref, wv_ref, pq_ref, pk_ref, pv_ref, ab_ref, win_sc)
    tail = (s_sc, z_ref, onorm_ref, o_ref)

    def save(hp, vals):
        u_sc[hp], wq_sc[hp], kt_sc[hp], at_sc[hp], egl_sc[hp] = vals

    def load(hp):
        return u_sc[hp], wq_sc[hp], kt_sc[hp], at_sc[hp], egl_sc[hp]

    def row(c):
        r0 = N_META + c * C
        return r0 if isinstance(c, int) else pl.multiple_of(r0, BF16_ROWS)

    heads = range(n_hp)

    def prepare(r0, c_len, first):
        return [_gdn_prepare_chunk(r0, c_len, first, hp, refs, consts) for hp in heads]

    def apply(r0, c_len, vals):
        return [_gdn_apply_chunk(r0, c_len, hp, vals[hp], *tail) for hp in heads]

    s_sc[...] = s0_ref[0]
    _interleave(apply(0, N_META, _interleave(prepare(0, N_META, True))))
    for hp, vals in enumerate(_interleave(prepare(N_META, C, False))):
        save(hp, vals)

    def body(c, carry):
        vals = [load(hp) for hp in heads]
        out = _interleave(apply(row(c), C, vals) + prepare(row(c + 1), C, False))
        for hp in heads:
            save(hp, out[n_hp + hp])
        return carry

    lax.fori_loop(0, n_chunks - 1, body, 0)
    _interleave(apply(row(n_chunks - 1), C, [load(hp) for hp in heads]))
    sout_ref[0] = s_sc[...]


def _gdn_core(qkv, conv_w, conv_prev, ab, z, a_log, dt_bias, out_norm, S0):
    B, L, _ = qkv.shape
    n_chunks = (L - N_META) // GDN_CHUNK
    assert n_chunks >= 1 and N_META + n_chunks * GDN_CHUNK == L
    n_hp = 4
    ng = GDN_HEADS // n_hp
    wd = n_hp * LANE
    C = GDN_CHUNK
    col = lambda off: pl.BlockSpec((1, L, wd), lambda b, h: (b, 0, off + h))
    wcol = lambda off: pl.BlockSpec((GDN_CONV, wd), lambda b, h: (0, off + h))
    pcol = lambda off: pl.BlockSpec((1, GDN_CONV - 1, wd), lambda b, h: (b, 0, off + h))
    state = pl.BlockSpec((1, n_hp, GDN_DK, GDN_DV), lambda b, h: (b, h, 0, 0))
    return pl.pallas_call(
        functools.partial(_gdn_core_kernel, n_chunks=n_chunks, n_hp=n_hp),
        grid=(B, ng),
        in_specs=[col(0), col(ng), col(2 * ng), wcol(0), wcol(ng), wcol(2 * ng), pcol(0), pcol(ng), pcol(2 * ng),
                  pl.BlockSpec((1, L, LANE), lambda b, h: (b, 0, 0)), col(0),
                  _full((1, LANE)), _full((1, LANE)), _full((1, GDN_DV)), state],
        out_specs=[col(0), state],
        out_shape=[jax.ShapeDtypeStruct((B, L, GDN_VAL), BF16),
                   jax.ShapeDtypeStruct((B, GDN_HEADS, GDN_DK, GDN_DV), F32)],
        scratch_shapes=[pltpu.VMEM((3 * n_hp, BF16_ROWS + N_META, LANE), F32),
                        pltpu.VMEM((n_hp, GDN_DK, GDN_DV), F32),
                        pltpu.VMEM((n_hp, C, GDN_DV), F32), pltpu.VMEM((n_hp, 2 * C, GDN_DK), BF16),
                        pltpu.VMEM((n_hp, GDN_DK, C), BF16), pltpu.VMEM((n_hp, C, C), BF16),
                        pltpu.VMEM((n_hp, SUBLANE, LANE), F32)],
        compiler_params=_cparams(("parallel", "parallel")),
        name="gdn_core",
    )(qkv, qkv, qkv, conv_w, conv_w, conv_w, conv_prev, conv_prev, conv_prev, ab, z, a_log, dt_bias, out_norm, S0)


def _gdn_step_kernel(new_ref, prev_ref, w_ref, ab_ref, z_ref, alog_ref, dtb_ref, onorm_ref, s_ref, o_ref, sout_ref):
    w = w_ref[...]
    prev = prev_ref[0]
    y = new_ref[0] * w[GDN_CONV - 1:GDN_CONV]
    for j in range(GDN_CONV - 1):
        y = y + prev[j:j + 1] * w[j:j + 1]
    y = _silu(y)
    ab = ab_ref[0]
    z = z_ref[0]
    alog = alog_ref[...]
    dtb = dtb_ref[...]
    ri = lax.broadcasted_iota(jnp.int32, (GDN_DK, GDN_DK), 0)
    ci = lax.broadcasted_iota(jnp.int32, (GDN_DK, GDN_DK), 1)
    eye = ri == ci
    row8 = lax.broadcasted_iota(jnp.int32, (SUBLANE, GDN_DK), 0)

    def l2n(t):
        return t * lax.rsqrt(jnp.sum(t * t, axis=-1, keepdims=True) + NORM_EPS)

    for h in range(GDN_HEADS):
        q = l2n(y[:, h * GDN_DK:(h + 1) * GDN_DK]) * (GDN_DK ** -0.5)
        k = l2n(y[:, GDN_KEY + h * GDN_DK:GDN_KEY + (h + 1) * GDN_DK])
        v = y[:, 2 * GDN_KEY + h * GDN_DV:2 * GDN_KEY + (h + 1) * GDN_DV]
        g = -jnp.exp(alog[:, h:h + 1]) * _softplus(ab[:, h:h + 1] + dtb[:, h:h + 1])
        beta = _sigmoid(ab[:, GDN_HEADS + h:GDN_HEADS + h + 1])
        eg = jnp.exp(g)
        S = s_ref[0, h]
        kq = jnp.where(row8 == 0, jnp.broadcast_to(k, (SUBLANE, GDN_DK)),
                       jnp.where(row8 == 1, jnp.broadcast_to(q, (SUBLANE, GDN_DK)), 0.0))
        kqS = _dot_hi(kq, S)
        v_new = beta * (v - eg * kqS[0:1])
        qk = jnp.sum(q * k, axis=-1, keepdims=True)
        o = eg * kqS[1:2] + qk * v_new
        k_col = jnp.sum(jnp.where(eye, jnp.broadcast_to(k, (GDN_DK, GDN_DK)), 0.0), axis=1, keepdims=True)
        sout_ref[0, h] = S * eg + k_col * v_new
        o = o * lax.rsqrt(jnp.mean(o * o, axis=-1, keepdims=True) + NORM_EPS) * onorm_ref[...]
        o = o * _silu(z[:, h * GDN_DV:(h + 1) * GDN_DV])
        o_ref[0, :, h * GDN_DV:(h + 1) * GDN_DV] = o.astype(o_ref.dtype)


def _gdn_step(qkv_new, conv_prev, conv_w, ab, z, a_log, dt_bias, out_norm, S0):
    Bs = qkv_new.shape[0]
    per = lambda *s: pl.BlockSpec((1,) + s, lambda b: (b,) + (0,) * len(s))
    return pl.pallas_call(
        _gdn_step_kernel,
        grid=(Bs,),
        in_specs=[per(1, GDN_QKV), per(GDN_CONV - 1, GDN_QKV), _full((GDN_CONV, GDN_QKV)), per(1, LANE),
                  per(1, GDN_VAL), _full((1, LANE)), _full((1, LANE)), _full((1, GDN_DV)),
                  per(GDN_HEADS, GDN_DK, GDN_DV)],
        out_specs=[per(1, GDN_VAL), per(GDN_HEADS, GDN_DK, GDN_DV)],
        out_shape=[jax.ShapeDtypeStruct((Bs, 1, GDN_VAL), BF16),
                   jax.ShapeDtypeStruct((Bs, GDN_HEADS, GDN_DK, GDN_DV), F32)],
        compiler_params=_cparams(("parallel",)),
        name="gdn_step",
    )(qkv_new, conv_prev, conv_w, ab, z, a_log, dt_bias, out_norm, S0)


def _proj_residual_kernel(x_ref, o_ref, w_ref, g_ref, out_ref):
    y = _dot(o_ref[0], w_ref[...])
    out_ref[0] = x_ref[0] + _rms(y, g_ref[...])


def _proj_residual(x, o, w, g):
    B, L, D = x.shape
    K = o.shape[-1]
    tl = _row_tile(L, 700, BF16_ROWS)
    return pl.pallas_call(
        _proj_residual_kernel,
        grid=(B, L // tl),
        in_specs=[pl.BlockSpec((1, tl, D), lambda b, l: (b, l, 0)), pl.BlockSpec((1, tl, K), lambda b, l: (b, l, 0)),
                  _full((K, D)), _full((1, D))],
        out_specs=pl.BlockSpec((1, tl, D), lambda b, l: (b, l, 0)),
        out_shape=jax.ShapeDtypeStruct((B, L, D), F32),
        compiler_params=_cparams(("parallel", "parallel")),
        name="proj_residual",
    )(x, o, w, g)


def _ffn_act(j, u, tap1, tap0, cw_ref, cb_ref):
    cw = cw_ref[j]
    conv = u * cw[2:3] + tap1 * cw[1:2] + tap0 * cw[0:1] + cb_ref[j]
    return (_silu(conv[:, :FFN_CHUNK]) * conv[:, FFN_CHUNK:]).astype(BF16)


def _ffn_seq_kernel(x_ref, gpre_ref, wup_ref, cw_ref, cb_ref, wdn_ref, gpost_ref, prev_ref, out_ref, newprev_ref,
                    h_sc, acc_sc, a_sc, u_sc, carry_sc, *, tl):
    x = x_ref[0]
    h_sc[...] = _rms(x, gpre_ref[...]).astype(BF16)
    acc_sc[...] = jnp.zeros_like(acc_sc)

    @pl.when(pl.program_id(1) == 0)
    def _():
        carry_sc[...] = jnp.zeros_like(carry_sc)
        carry_sc[:, SUBLANE - (FFN_CONV - 1):SUBLANE, :] = prev_ref[0]

    def up(j):
        return _dot(h_sc[...], wup_ref[j])

    def put(slot, j, u):
        u_sc[slot, 0:SUBLANE, :] = carry_sc[j]
        u_sc[slot, SUBLANE:SUBLANE + tl, :] = u
        carry_sc[j] = u[tl - SUBLANE:tl]
        newprev_ref[0, j] = u[tl - (FFN_CONV - 1):tl]

    def act_down(slot, j):
        a_sc[slot] = _ffn_act(j, u_sc[slot, SUBLANE:SUBLANE + tl, :], u_sc[slot, SUBLANE - 1:SUBLANE - 1 + tl, :],
                              u_sc[slot, SUBLANE - 2:SUBLANE - 2 + tl, :], cw_ref, cb_ref)
        acc_sc[...] += _dot(a_sc[slot], wdn_ref[j])

    def step(slot, j):
        u_next = up(j + 1)
        act_down(slot, j)
        put(1 - slot, j + 1, u_next)

    put(0, 0, up(0))

    def body(jj, carry):
        step(0, 2 * jj)
        step(1, 2 * jj + 1)
        return carry

    n_pairs = (FFN_NCH - 1) // 2
    lax.fori_loop(0, n_pairs, body, 0)
    for j in range(2 * n_pairs, FFN_NCH - 1):
        step(j % 2, j)
    act_down((FFN_NCH - 1) % 2, FFN_NCH - 1)
    out_ref[0] = x + _rms(acc_sc[...], gpost_ref[...])


def _ffn_step_kernel(x_ref, gpre_ref, wup_ref, cw_ref, cb_ref, wdn_ref, gpost_ref, prev_ref, out_ref, u_ref,
                     h_sc, acc_sc):
    x = x_ref[0]
    h_sc[...] = _rms(x, gpre_ref[...]).astype(BF16)
    acc_sc[...] = jnp.zeros_like(acc_sc)

    def body(j, carry):
        u = _dot(h_sc[...], wup_ref[j])
        u_ref[j] = u
        acc_sc[...] += _dot(_ffn_act(j, u, prev_ref[j, 1], prev_ref[j, 0], cw_ref, cb_ref), wdn_ref[j])
        return carry

    lax.fori_loop(0, FFN_NCH, body, 0)
    out_ref[0] = x + _rms(acc_sc[...], gpost_ref[...])


def _ffn_weight_specs():
    w2 = 2 * FFN_CHUNK
    once = pl.Buffered(1)
    return [_full((1, D_MODEL)), _full((FFN_NCH, D_MODEL, w2), once), _full((FFN_NCH, FFN_CONV, w2)),
            _full((FFN_NCH, 1, w2)), _full((FFN_NCH, FFN_CHUNK, D_MODEL), once), _full((1, D_MODEL))]


def _chunk_cols(t):
    lead = t.shape[:-1]
    t = t.reshape(lead + (2, FFN_NCH, FFN_CHUNK))
    t = jnp.moveaxis(t, -2, 0)
    return t.reshape((FFN_NCH,) + lead + (2 * FFN_CHUNK,))


def _unchunk_cols(t):
    lead = t.shape[1:-1]
    t = t.reshape((FFN_NCH,) + lead + (2, FFN_CHUNK))
    t = jnp.moveaxis(t, 0, -2)
    return t.reshape(lead + (2 * D_FF,))


def _ffn_seq(x, prev, wts):
    B, L, D = x.shape
    tl = _row_tile(L, 700, BF16_ROWS)
    w2 = 2 * FFN_CHUNK
    prev_c = jnp.moveaxis(_chunk_cols(prev), 0, 1)
    out, newprev = pl.pallas_call(
        functools.partial(_ffn_seq_kernel, tl=tl),
        grid=(B, L // tl),
        in_specs=[pl.BlockSpec((1, tl, D), lambda b, l: (b, l, 0))] + _ffn_weight_specs()
        + [pl.BlockSpec((1, FFN_NCH, FFN_CONV - 1, w2), lambda b, l: (b, 0, 0, 0))],
        out_specs=[pl.BlockSpec((1, tl, D), lambda b, l: (b, l, 0)),
                   pl.BlockSpec((1, FFN_NCH, FFN_CONV - 1, w2), lambda b, l: (b, 0, 0, 0))],
        out_shape=[jax.ShapeDtypeStruct((B, L, D), F32), jax.ShapeDtypeStruct((B, FFN_NCH, FFN_CONV - 1, w2), F32)],
        scratch_shapes=[pltpu.VMEM((tl, D), BF16), pltpu.VMEM((tl, D), F32), pltpu.VMEM((2, tl, FFN_CHUNK), BF16),
                        pltpu.VMEM((2, SUBLANE + tl, w2), F32),
                        pltpu.VMEM((FFN_NCH, SUBLANE, w2), F32)],
        compiler_params=_cparams(("parallel", "arbitrary")),
        name="ffn_seq",
    )(x, *wts, prev_c)
    return out, _unchunk_cols(jnp.moveaxis(newprev, 1, 0))


def _ffn_step(x, prev, wts):
    _, Bs, D = x.shape
    w2 = 2 * FFN_CHUNK
    prev_c = _chunk_cols(jnp.moveaxis(prev, 1, 0))
    out, u = pl.pallas_call(
        _ffn_step_kernel,
        grid=(1,),
        in_specs=[_full((1, Bs, D))] + _ffn_weight_specs() + [_full((FFN_NCH, FFN_CONV - 1, Bs, w2))],
        out_specs=[_full((1, Bs, D)), _full((FFN_NCH, Bs, w2))],
        out_shape=[jax.ShapeDtypeStruct((1, Bs, D), F32), jax.ShapeDtypeStruct((FFN_NCH, Bs, w2), F32)],
        scratch_shapes=[pltpu.VMEM((Bs, D), BF16), pltpu.VMEM((Bs, D), F32)],
        compiler_params=_cparams(("arbitrary",)),
        name="ffn_step",
    )(x, *wts, prev_c)
    return out, jnp.concatenate([prev[:, 1:], _unchunk_cols(u)[:, None]], axis=1)


def _rot_half(t, width):
    lane = lax.broadcasted_iota(jnp.int32, t.shape, t.ndim - 1)
    first = (lane % MLA_ROPE) < (MLA_ROPE // 2)
    return jnp.where(first, pltpu.roll(t, width - MLA_ROPE // 2, t.ndim - 1), pltpu.roll(t, MLA_ROPE // 2, t.ndim - 1))


def _mla_proj_kernel(x_ref, gkv_ref, wkva_ref, gkva_ref, gq_ref, wqa_ref, gqa_ref, wqb_ref, wuk_ref, cs_ref,
                     c_ref, kr_ref, kcat_ref, q_ref):
    x = x_ref[0]
    cos = cs_ref[:, 0:LANE]
    sin = cs_ref[:, LANE:2 * LANE]
    lane = lax.broadcasted_iota(jnp.int32, (1, LANE), 1)

    ckv = _dot(_rms(x, gkv_ref[...]).astype(BF16), wkva_ref[...])
    c = _rms(ckv[:, :MLA_KV_LORA], gkva_ref[...])
    r = ckv[:, MLA_KV_LORA:MLA_QK]
    kr = r * cos + _rot_half(r, LANE) * sin
    c_ref[0] = c
    kr_ref[0] = kr[:, :MLA_ROPE]
    kcat_ref[0, :, 0:MLA_KV_LORA] = c.astype(BF16)
    kcat_ref[0, :, MLA_KV_LORA:MLA_QK] = kr.astype(BF16)

    qa = _dot(_rms(x, gq_ref[...]).astype(BF16), wqa_ref[...])
    qb = _dot(_rms(qa, gqa_ref[...]).astype(BF16), wqb_ref[...])
    n_nope = MLA_HEADS * MLA_NOPE
    n_pe = MLA_HEADS * MLA_ROPE
    pe = qb[:, n_nope:n_nope + n_pe]
    reps = n_pe // LANE
    pe = pe * jnp.concatenate([cos] * reps, axis=1) + _rot_half(pe, n_pe) * jnp.concatenate([sin] * reps, axis=1)
    for h in range(MLA_HEADS):
        q_lat = _dot(qb[:, h * MLA_NOPE:(h + 1) * MLA_NOPE].astype(BF16), wuk_ref[h])
        q_ref[0, h, :, 0:MLA_KV_LORA] = (q_lat * MLA_SCALE).astype(BF16)
        t = pe[:, (h // 2) * LANE:(h // 2 + 1) * LANE]
        if h % 2 == 1:
            t = pltpu.roll(t, MLA_ROPE, 1)
        q_ref[0, h, :, MLA_KV_LORA:MLA_QK] = jnp.where(lane < MLA_ROPE, t * MLA_SCALE, 0.0).astype(BF16)


def _mla_proj(x, cs, gkv, wkva, gkva, gq, wqa, gqa, wqb, wuk):
    B, L, D = x.shape
    tl = _row_tile(L, 700, BF16_ROWS)
    row = lambda n: pl.BlockSpec((1, tl, n), lambda b, l: (b, l, 0))
    return pl.pallas_call(
        _mla_proj_kernel,
        grid=(B, L // tl),
        in_specs=[row(D), _full((1, D)), _full((D, MLA_QK)), _full((1, MLA_KV_LORA)), _full((1, D)),
                  _full((D, MLA_Q_LORA)), _full((1, MLA_Q_LORA)), _full(wqb.shape), _full(wuk.shape),
                  pl.BlockSpec((tl, 2 * LANE), lambda b, l: (l, 0))],
        out_specs=[row(MLA_KV_LORA), row(MLA_ROPE), row(MLA_QK),
                   pl.BlockSpec((1, MLA_HEADS, tl, MLA_QK), lambda b, l: (b, 0, l, 0))],
        out_shape=[jax.ShapeDtypeStruct((B, L, MLA_KV_LORA), F32), jax.ShapeDtypeStruct((B, L, MLA_ROPE), F32),
                   jax.ShapeDtypeStruct((B, L, MLA_QK), BF16),
                   jax.ShapeDtypeStruct((B, MLA_HEADS, L, MLA_QK), BF16)],
        compiler_params=_cparams(("parallel", "parallel")),
        name="mla_proj",
    )(x, gkv, wkva, gkva, gq, wqa, gqa, wqb, wuk, cs)


def _attn_kernel(q_ref, kcat_ref, o_ref, m_sc, l_sc, acc_sc, *, tq, n_hp):
    i = pl.program_id(1)
    m_sc[...] = jnp.full_like(m_sc, -jnp.inf)
    l_sc[...] = jnp.zeros_like(l_sc)
    acc_sc[...] = jnp.zeros_like(acc_sc)
    ri = lax.broadcasted_iota(jnp.int32, (tq, tq), 0)
    ci = lax.broadcasted_iota(jnp.int32, (tq, tq), 1)

    def head_tile(hp, kt, masked):
        kv = kcat_ref[0, pl.ds(pl.multiple_of(kt * tq, BF16_ROWS), tq), :]
        s = _dot_nt(q_ref[0, hp], kv)
        yield
        if masked:
            s = jnp.where(ci <= ri, s, -jnp.inf)
        m_prev = m_sc[hp]
        m_new = jnp.maximum(m_prev, jnp.max(s, axis=1, keepdims=True))
        alpha = jnp.exp(m_prev - m_new)
        p = jnp.exp(s - m_new)
        l_sc[hp] = alpha * l_sc[hp] + jnp.sum(p, axis=1, keepdims=True)
        m_sc[hp] = m_new
        pv = _dot(p.astype(BF16), kv[:, :MLA_KV_LORA])
        yield
        acc_sc[hp] = alpha * acc_sc[hp] + pv

    def tiles(kt, masked):
        _interleave([head_tile(hp, kt, masked) for hp in range(n_hp)])

    def body(kt, carry):
        tiles(kt, False)
        return carry

    lax.fori_loop(0, i, body, 0)
    tiles(i, True)
    for hp in range(n_hp):
        o_ref[0, hp] = (acc_sc[hp] / l_sc[hp]).astype(o_ref.dtype)


def _attn_seq(q, kcat):
    B, nh, L, _ = q.shape
    tq = _row_tile(L, 700, BF16_ROWS)
    n_hp = 2
    return pl.pallas_call(
        functools.partial(_attn_kernel, tq=tq, n_hp=n_hp),
        grid=(B, L // tq, nh // n_hp),
        in_specs=[pl.BlockSpec((1, n_hp, tq, MLA_QK), lambda b, i, h: (b, h, i, 0)),
                  pl.BlockSpec((1, L, MLA_QK), lambda b, i, h: (b, 0, 0))],
        out_specs=pl.BlockSpec((1, n_hp, tq, MLA_KV_LORA), lambda b, i, h: (b, h, i, 0)),
        out_shape=jax.ShapeDtypeStruct((B, nh, L, MLA_KV_LORA), BF16),
        scratch_shapes=[pltpu.VMEM((n_hp, tq, 1), F32), pltpu.VMEM((n_hp, tq, 1), F32),
                        pltpu.VMEM((n_hp, tq, MLA_KV_LORA), F32)],
        compiler_params=_cparams(("parallel", "parallel", "parallel")),
        name="attn_seq",
    )(q, kcat)


def _attn_paged_kernel(pt_ref, q_ref, kcat_ref, cache_c_ref, cache_k_ref, o_ref, cbuf, kbuf, c16, s_sc, sem,
                       *, n_pages, page, kchunk):
    b = pl.program_id(0)
    nb = pl.num_programs(0)
    P = n_pages * page

    def copies(bb, slot):
        out = []
        for pg in range(n_pages):
            idx = pt_ref[bb, pg]
            out.append(pltpu.make_async_copy(cache_c_ref.at[idx], cbuf.at[slot, pl.ds(pg * page, page), :],
                                             sem.at[0, slot]))
            out.append(pltpu.make_async_copy(cache_k_ref.at[idx], kbuf.at[slot, :, pl.ds(pg * page, page)],
                                             sem.at[1, slot]))
        return out

    @pl.when(b == 0)
    def _():
        for cp in copies(0, 0):
            cp.start()

    @pl.when(b + 1 < nb)
    def _():
        for cp in copies(b + 1, (b + 1) % 2):
            cp.start()

    slot = b % 2
    for cp in copies(b, slot):
        cp.wait()

    q = q_ref[0]
    q_lat = q[:, :MLA_KV_LORA]
    q_pe = q[:, MLA_KV_LORA:MLA_KV_LORA + MLA_ROPE]
    for c0 in range(0, P, kchunk):
        cc = cbuf[slot, c0:c0 + kchunk, :].astype(BF16)
        c16[c0:c0 + kchunk, :] = cc
        kk = kbuf[slot, :, c0:c0 + kchunk].astype(BF16)
        s_sc[:, c0:c0 + kchunk] = _dot_nt(q_lat, cc) + _dot(q_pe, kk)
    kself = kcat_ref[0]
    s_self = jnp.sum(q.astype(F32) * kself.astype(F32), axis=-1, keepdims=True)
    s = s_sc[...]
    m = jnp.maximum(jnp.max(s, axis=-1, keepdims=True), s_self)
    p = jnp.exp(s - m)
    p_self = jnp.exp(s_self - m)
    denom = jnp.sum(p, axis=-1, keepdims=True) + p_self
    s_sc[...] = p
    acc = p_self * kself[:, :MLA_KV_LORA].astype(F32)
    for c0 in range(0, P, kchunk):
        acc = acc + _dot(s_sc[:, c0:c0 + kchunk].astype(BF16), c16[c0:c0 + kchunk, :])
    o_ref[0] = (acc / denom).astype(o_ref.dtype)


def _attn_paged(q, kcat, cache_c, cache_k, page_table):
    Bs, nh, _ = q.shape
    n_pages = page_table.shape[1]
    page = cache_c.shape[1]
    cache_k = jnp.swapaxes(cache_k, 1, 2)
    P = n_pages * page
    kchunk = _row_tile(P, 1024, LANE)
    grid_spec = pltpu.PrefetchScalarGridSpec(
        num_scalar_prefetch=1,
        grid=(Bs,),
        in_specs=[pl.BlockSpec((1, nh, MLA_QK), lambda b, pt: (b, 0, 0)),
                  pl.BlockSpec((1, 1, MLA_QK), lambda b, pt: (b, 0, 0)),
                  pl.BlockSpec(memory_space=pl.ANY), pl.BlockSpec(memory_space=pl.ANY)],
        out_specs=pl.BlockSpec((1, nh, MLA_KV_LORA), lambda b, pt: (b, 0, 0)),
        scratch_shapes=[pltpu.VMEM((2, P, MLA_KV_LORA), F32), pltpu.VMEM((2, MLA_ROPE, P), F32),
                        pltpu.VMEM((P, MLA_KV_LORA), BF16), pltpu.VMEM((nh, P), F32),
                        pltpu.SemaphoreType.DMA((2, 2))],
    )
    return pl.pallas_call(
        functools.partial(_attn_paged_kernel, n_pages=n_pages, page=page, kchunk=kchunk),
        grid_spec=grid_spec,
        out_shape=jax.ShapeDtypeStruct((Bs, nh, MLA_KV_LORA), BF16),
        compiler_params=_cparams(("arbitrary",)),
        name="attn_paged",
    )(page_table, q, kcat, cache_c, cache_k)


def _mla_out_kernel(x_ref, o_ref, wuv_ref, wout_ref, g_ref, out_ref, acc_sc):
    for h in range(MLA_HEADS):
        oh = _dot(o_ref[0, h], wuv_ref[h]).astype(BF16)
        y = _dot(oh, wout_ref[h])
        if h == 0:
            acc_sc[...] = y
        else:
            acc_sc[...] += y
    out_ref[0] = x_ref[0] + _rms(acc_sc[...], g_ref[...])


def _mla_out(x, o_lat, wuv, wout, g):
    B, L, D = x.shape
    tl = _row_tile(L, 700, BF16_ROWS)
    return pl.pallas_call(
        _mla_out_kernel,
        grid=(B, L // tl),
        in_specs=[pl.BlockSpec((1, tl, D), lambda b, l: (b, l, 0)),
                  pl.BlockSpec((1, MLA_HEADS, tl, MLA_KV_LORA), lambda b, l: (b, 0, l, 0)),
                  _full(wuv.shape), _full(wout.shape), _full((1, D))],
        out_specs=pl.BlockSpec((1, tl, D), lambda b, l: (b, l, 0)),
        out_shape=jax.ShapeDtypeStruct((B, L, D), F32),
        scratch_shapes=[pltpu.VMEM((tl, D), F32)],
        compiler_params=_cparams(("parallel", "parallel")),
        name="mla_out",
    )(x, o_lat, wuv, wout, g)


def _rope_table(positions):
    half = MLA_ROPE // 2
    inv = ROPE_THETA ** (-jnp.arange(half, dtype=F32) / half)
    ang = positions.astype(F32)[:, None] * inv[None, :]
    cos, sin = jnp.cos(ang), jnp.sin(ang)
    reps = LANE // MLA_ROPE
    return jnp.concatenate([jnp.concatenate([cos, cos] * reps, axis=1), jnp.concatenate([-sin, sin] * reps, axis=1)],
                           axis=1)


def _row(v, width=None):
    v = v.reshape(1, -1).astype(F32)
    if width is not None and v.shape[1] < width:
        v = jnp.pad(v, ((0, 0), (0, width - v.shape[1])))
    return v


def _prepare(p):
    i = 0
    w_in = p['a_w_in'][i]
    n_ab = 2 * GDN_HEADS
    wq_b = p['b_w_q_b'][i].reshape(MLA_Q_LORA, MLA_HEADS, MLA_NOPE + MLA_ROPE)
    wq_b = jnp.concatenate([wq_b[:, :, :MLA_NOPE].reshape(MLA_Q_LORA, -1), wq_b[:, :, MLA_NOPE:].reshape(MLA_Q_LORA, -1)],
                           axis=1)
    prep = {
        'a_norm_pre': _row(p['a_norm_pre'][i]), 'a_norm_post': _row(p['a_norm_post'][i]),
        'a_wqkv': w_in[:, :GDN_QKV].astype(BF16), 'a_wz': w_in[:, GDN_QKV:GDN_QKV + GDN_VAL].astype(BF16),
        'a_wab': jnp.pad(w_in[:, GDN_QKV + GDN_VAL:], ((0, 0), (0, LANE - n_ab))).astype(BF16),
        'a_conv_w': p['a_conv_w'][i].astype(F32), 'a_log': _row(p['a_log'][i], LANE),
        'a_dt_bias': _row(p['a_dt_bias'][i], LANE), 'a_out_norm': _row(p['a_out_norm'][i]),
        'a_w_out': p['a_w_out'][i].astype(BF16),
        'kv_norm': _row(p['kv_norm']),
        'kv_w_a': jnp.pad(p['kv_w_a'], ((0, 0), (0, MLA_QK - MLA_KV_LORA - MLA_ROPE))).astype(BF16),
        'kv_a_norm': _row(p['kv_a_norm']),
        'kv_w_uk': jnp.transpose(p['kv_w_uk'], (1, 2, 0)).astype(BF16),
        'kv_w_uv': jnp.transpose(p['kv_w_uv'], (1, 0, 2)).astype(BF16),
        'b_norm_pre': _row(p['b_norm_pre'][i]), 'b_norm_post': _row(p['b_norm_post'][i]),
        'b_w_q_a': p['b_w_q_a'][i].astype(BF16), 'b_q_a_norm': _row(p['b_q_a_norm'][i]),
        'b_w_q_b': wq_b.astype(BF16),
        'b_w_out': p['b_w_out'][i].reshape(MLA_HEADS, MLA_V, D_MODEL).astype(BF16),
        'ffn': [],
    }
    for layer in range(2):
        w_up = jnp.moveaxis(_chunk_cols(p['f_w_up'][layer]), 0, 0).astype(BF16)
        prep['ffn'].append((
            _row(p['f_norm_pre'][layer]), w_up, _chunk_cols(p['f_conv_w'][layer]).astype(F32),
            _chunk_cols(p['f_conv_b'][layer].reshape(1, -1)).astype(F32),
            p['f_w_down'][layer].reshape(FFN_NCH, FFN_CHUNK, D_MODEL).astype(BF16), _row(p['f_norm_post'][layer])))
    return prep


def _mla_block(x, cs, prep):
    return _mla_proj(x, cs, prep['kv_norm'], prep['kv_w_a'], prep['kv_a_norm'], prep['b_norm_pre'], prep['b_w_q_a'],
                     prep['b_q_a_norm'], prep['b_w_q_b'], prep['kv_w_uk'])


def _trunk_seq(x, S0, dconv0, fconv0, prep):
    B, L, _ = x.shape
    assert L >= SUBLANE
    qkv, z, ab, tail = _gdn_in(x, prep['a_norm_pre'], prep['a_wqkv'], prep['a_wz'], prep['a_wab'], BF16)
    o, S = _gdn_core(qkv, prep['a_conv_w'], dconv0[0], ab, z, prep['a_log'], prep['a_dt_bias'], prep['a_out_norm'], S0[0])
    dconv = tail[:, SUBLANE - (GDN_CONV - 1):]
    x = _proj_residual(x, o, prep['a_w_out'], prep['a_norm_post'])
    x, fconv_a = _ffn_seq(x, fconv0[0], prep['ffn'][0])
    cs = _rope_table(jnp.arange(L, dtype=jnp.int32))
    c, kr, kcat, q = _mla_block(x, cs, prep)
    o_lat = _attn_seq(q, kcat)
    x = _mla_out(x, o_lat, prep['kv_w_uv'], prep['b_w_out'], prep['b_norm_post'])
    x, fconv_b = _ffn_seq(x, fconv0[1], prep['ffn'][1])
    return x, S[None], dconv[None], jnp.stack([fconv_a, fconv_b]), c, kr


def _trunk_step(x, pos, S0, dconv0, fconv0, cache_c, cache_k, page_table, prep):
    Bs = x.shape[0]
    xt = x.reshape(1, Bs, D_MODEL)
    qkv, z, ab, _ = _gdn_in(xt, prep['a_norm_pre'], prep['a_wqkv'], prep['a_wz'], prep['a_wab'], F32)
    o, S = _gdn_step(qkv.reshape(Bs, 1, -1), dconv0[0], prep['a_conv_w'], ab.reshape(Bs, 1, -1), z.reshape(Bs, 1, -1),
                     prep['a_log'], prep['a_dt_bias'], prep['a_out_norm'], S0[0])
    dconv = jnp.concatenate([dconv0[0][:, 1:], qkv.reshape(Bs, 1, -1)], axis=1)
    xt = _proj_residual(xt, o.reshape(1, Bs, -1), prep['a_w_out'], prep['a_norm_post'])
    xt, fconv_a = _ffn_step(xt, fconv0[0], prep['ffn'][0])
    cs = _rope_table(jnp.full((Bs,), pos, jnp.int32))
    c, kr, kcat, q = _mla_block(xt, cs, prep)
    q = jnp.moveaxis(q[0], 0, 1)
    o_lat = _attn_paged(q, kcat.reshape(Bs, 1, MLA_QK), cache_c, cache_k, page_table)
    o_lat = jnp.moveaxis(o_lat, 0, 1)[None]
    xt = _mla_out(xt, o_lat, prep['kv_w_uv'], prep['b_w_out'], prep['b_norm_post'])
    xt, fconv_b = _ffn_step(xt, fconv0[1], prep['ffn'][1])
    return (xt.reshape(Bs, 1, D_MODEL), S[None], dconv[None], jnp.stack([fconv_a, fconv_b]),
            c.reshape(Bs, 1, MLA_KV_LORA), kr.reshape(Bs, 1, MLA_ROPE))


def kernel(x_prompt, x_sample, state_delta_S, state_delta_conv, state_ffn_conv, cache_kv_latent, cache_k_rope,
           page_table, meta_tokens, a_norm_pre, a_norm_post, a_w_in, a_conv_w, a_log, a_dt_bias, a_out_norm, a_w_out,
           kv_norm, kv_w_a, kv_a_norm, kv_w_uk, kv_w_uv, b_norm_pre, b_norm_post, b_w_q_a, b_q_a_norm, b_w_q_b,
           b_w_out, f_norm_pre, f_norm_post, f_w_up, f_conv_w, f_conv_b, f_w_down):
    prep = _prepare({
        'a_norm_pre': a_norm_pre, 'a_norm_post': a_norm_post, 'a_w_in': a_w_in, 'a_conv_w': a_conv_w, 'a_log': a_log,
        'a_dt_bias': a_dt_bias, 'a_out_norm': a_out_norm, 'a_w_out': a_w_out, 'kv_norm': kv_norm, 'kv_w_a': kv_w_a,
        'kv_a_norm': kv_a_norm, 'kv_w_uk': kv_w_uk, 'kv_w_uv': kv_w_uv, 'b_norm_pre': b_norm_pre,
        'b_norm_post': b_norm_post, 'b_w_q_a': b_w_q_a, 'b_q_a_norm': b_q_a_norm, 'b_w_q_b': b_w_q_b,
        'b_w_out': b_w_out, 'f_norm_pre': f_norm_pre, 'f_norm_post': f_norm_post, 'f_w_up': f_w_up,
        'f_conv_w': f_conv_w, 'f_conv_b': f_conv_b, 'f_w_down': f_w_down})

    bp = x_prompt.shape[0]
    xp = jnp.concatenate([jnp.broadcast_to(meta_tokens.astype(x_prompt.dtype)[None], (bp, N_META, D_MODEL)), x_prompt],
                         axis=1)
    zS = jnp.zeros((1, bp, GDN_HEADS, GDN_DK, GDN_DV), state_delta_S.dtype)
    zdc = jnp.zeros((1, bp, GDN_CONV - 1, GDN_QKV), x_prompt.dtype)
    zfc = jnp.zeros((2, bp, FFN_CONV - 1, 2 * D_FF), x_prompt.dtype)
    yp, p_S, p_dconv, p_fconv, p_c, p_kr = _trunk_seq(xp, zS, zdc, zfc, prep)

    past_len = page_table.shape[1] * cache_kv_latent.shape[1]
    ys, s_S, s_dconv, s_fconv, s_c, s_kr = _trunk_step(x_sample, past_len, state_delta_S, state_delta_conv,
                                                       state_ffn_conv, cache_kv_latent, cache_k_rope, page_table, prep)
    return (yp[:, N_META:], ys, p_S, p_dconv, p_fconv, p_c, p_kr, s_S, s_dconv, s_fconv, s_c, s_kr)
```

```python
import functools
import math

import jax
import jax.numpy as jnp
from jax import lax
from jax.experimental import pallas as pl
from jax.experimental.pallas import tpu as pltpu

F32 = jnp.float32
BF16 = jnp.bfloat16

D_MODEL = 1024
N_META = 16
NORM_EPS = 1e-6

GDN_HEADS = 8
GDN_DK = 128
GDN_DV = 128
GDN_CONV = 4
GDN_CHUNK = 64
GDN_KEY = GDN_HEADS * GDN_DK
GDN_VAL = GDN_HEADS * GDN_DV
GDN_QKV = 2 * GDN_KEY + GDN_VAL
GDN_SOLVE_SPLIT = 3

MLA_HEADS = 8
MLA_Q_LORA = 384
MLA_KV_LORA = 256
MLA_NOPE = 128
MLA_ROPE = 64
MLA_V = 128
MLA_SCALE = 1.0 / math.sqrt(MLA_NOPE + MLA_ROPE)
ROPE_THETA = 10000.0
MLA_QK = MLA_KV_LORA + 128

D_FF = 2816
FFN_CONV = 3
FFN_CHUNK = 256
FFN_NCH = D_FF // FFN_CHUNK

LANE = 128
SUBLANE = 8
BF16_ROWS = 16
VMEM_LIMIT = 56 * 1024 * 1024

HI = lax.Precision.HIGHEST


def _row_tile(n, cap, align):
    best = None
    for t in range(align, min(n, cap) + 1, align):
        if n % t == 0:
            best = t
    return n if best is None else best


def _cparams(sem, flags=None):
    return pltpu.CompilerParams(dimension_semantics=sem, vmem_limit_bytes=VMEM_LIMIT, flags=flags)


def _rms(x, w):
    return x * lax.rsqrt(jnp.mean(x * x, axis=-1, keepdims=True) + NORM_EPS) * w


def _sigmoid(x):
    return 1.0 / (1.0 + jnp.exp(-x))


def _silu(x):
    return x * _sigmoid(x)


def _softplus(x):
    return jnp.maximum(x, 0.0) + jnp.log(1.0 + jnp.exp(-jnp.abs(x)))


def _dot(a, b):
    return jnp.dot(a, b, preferred_element_type=F32)


def _dot_nt(a, b, precision=None):
    return lax.dot_general(a, b, (((1,), (1,)), ((), ())), precision=precision, preferred_element_type=F32)


def _split_bf16(t):
    hi = t.astype(BF16)
    return hi, (t - hi.astype(F32)).astype(BF16)


def _dot_parts(a_parts, b_parts):
    out = _dot(a_parts[0], b_parts[0])
    if len(a_parts) > 1:
        out = out + (_dot(a_parts[0], b_parts[1]) + _dot(a_parts[1], b_parts[0]))
    return out


def _interleave(gens):
    results = [None] * len(gens)
    active = list(range(len(gens)))
    while active:
        for i in list(active):
            try:
                next(gens[i])
            except StopIteration as stop:
                results[i] = stop.value
                active.remove(i)
    return results


def _dot_hi(a, b):
    return jnp.dot(a, b, precision=HI, preferred_element_type=F32)


def _full(shape, pipeline_mode=None):
    nd = len(shape)
    return pl.BlockSpec(shape, lambda *_: (0,) * nd, pipeline_mode=pipeline_mode)


def _gdn_in_kernel(x_ref, g_ref, wqkv_ref, wz_ref, wab_ref, qkv_ref, z_ref, ab_ref, tail_ref, *, tl):
    h = _rms(x_ref[0], g_ref[...]).astype(BF16)
    for n0 in range(0, GDN_QKV, 512):
        r = _dot(h, wqkv_ref[:, n0:n0 + 512])
        qkv_ref[0, :, n0:n0 + 512] = r.astype(qkv_ref.dtype)
        tail_ref[0, :, n0:n0 + 512] = r[tl - SUBLANE:tl]
    for n0 in range(0, GDN_VAL, 512):
        z_ref[0, :, n0:n0 + 512] = _dot(h, wz_ref[:, n0:n0 + 512]).astype(z_ref.dtype)
    ab_ref[0] = _dot(h, wab_ref[...])


def _gdn_in(x, g, wqkv, wz, wab, act_dtype):
    B, L, D = x.shape
    tl = _row_tile(L, 700, BF16_ROWS)
    row = lambda n: pl.BlockSpec((1, tl, n), lambda b, l: (b, l, 0))
    return pl.pallas_call(
        functools.partial(_gdn_in_kernel, tl=tl),
        grid=(B, L // tl),
        in_specs=[row(D), _full((1, D)), _full((D, GDN_QKV)), _full((D, GDN_VAL)), _full((D, LANE))],
        out_specs=[row(GDN_QKV), row(GDN_VAL), row(LANE), pl.BlockSpec((1, SUBLANE, GDN_QKV), lambda b, l: (b, 0, 0))],
        out_shape=[jax.ShapeDtypeStruct((B, L, GDN_QKV), act_dtype), jax.ShapeDtypeStruct((B, L, GDN_VAL), act_dtype),
                   jax.ShapeDtypeStruct((B, L, LANE), F32), jax.ShapeDtypeStruct((B, SUBLANE, GDN_QKV), F32)],
        compiler_params=_cparams(("parallel", "arbitrary")),
        name="gdn_in",
    )(x, g, wqkv, wz, wab)


def _head_scalar(ref, sel):
    return jnp.sum(jnp.where(sel, ref[...], 0.0), axis=1, keepdims=True)


def _gdn_prepare_chunk(r0, C, first, hp, refs, consts):
    (q_ref, k_ref, v_ref, wq_ref, wk_ref, wv_ref, pq_ref, pk_ref, pv_ref, ab_ref, win_sc) = refs
    sel_a, sel_b, neg_ea, dtb = consts[hp]
    lanes = slice(hp * LANE, (hp + 1) * LANE)
    halo = BF16_ROWS

    def window(src_ref, prev_ref, slot):
        if first:
            win_sc[slot, 0:halo, :] = jnp.zeros((halo, LANE), F32)
            win_sc[slot, halo - (GDN_CONV - 1):halo, :] = prev_ref[0, :, lanes]
            win_sc[slot, halo:halo + C, :] = src_ref[0, 0:C, lanes].astype(F32)
            return win_sc[slot, 0:halo + C, :]
        return src_ref[0, pl.ds(r0 - halo, C + halo), lanes].astype(F32)

    def conv(src_ref, prev_ref, w_ref, slot):
        win = window(src_ref, prev_ref, slot)
        w = w_ref[:, lanes]
        acc = win[halo:halo + C] * w[GDN_CONV - 1:GDN_CONV]
        for j in range(GDN_CONV - 1):
            off = halo - (GDN_CONV - 1) + j
            acc = acc + win[off:off + C] * w[j:j + 1]
        return _silu(acc)

    def l2n(t):
        return t * lax.rsqrt(jnp.sum(t * t, axis=-1, keepdims=True) + NORM_EPS)

    q = l2n(conv(q_ref, pq_ref, wq_ref, 3 * hp)) * (GDN_DK ** -0.5)
    k = l2n(conv(k_ref, pk_ref, wk_ref, 3 * hp + 1))
    v = conv(v_ref, pv_ref, wv_ref, 3 * hp + 2)

    ab = ab_ref[0, pl.ds(r0, C), :]
    a_col = jnp.sum(jnp.where(sel_a, ab, 0.0), axis=1, keepdims=True)
    b_col = jnp.sum(jnp.where(sel_b, ab, 0.0), axis=1, keepdims=True)
    g_col = neg_ea * _softplus(a_col + dtb)
    beta = _sigmoid(b_col)

    ri = lax.broadcasted_iota(jnp.int32, (C, C), 0)
    ci = lax.broadcasted_iota(jnp.int32, (C, C), 1)
    eye = ri == ci
    causal = ri >= ci
    g_row = jnp.sum(jnp.where(eye, jnp.broadcast_to(g_col, (C, C)), 0.0), axis=0, keepdims=True)
    gc_col = jnp.sum(jnp.where(causal, jnp.broadcast_to(g_row, (C, C)), 0.0), axis=1, keepdims=True)
    gc_row = jnp.sum(jnp.where(eye, jnp.broadcast_to(gc_col, (C, C)), 0.0), axis=0, keepdims=True)
    decay = jnp.where(causal, jnp.exp(jnp.where(causal, gc_col - gc_row, 0.0)), 0.0)
    g_last = gc_row[:, C - 1:C]
    eg = jnp.exp(gc_col)

    kb = k * beta
    yield
    kqk = _dot_nt(jnp.concatenate([kb, q], axis=0).astype(BF16), k.astype(BF16))
    k_tail_t = jnp.transpose(k * jnp.exp(g_last - gc_col)).astype(BF16)
    yield
    A = jnp.where(ri > ci, kqk[:C] * decay, 0.0)
    attn = kqk[C:] * decay
    X = jnp.concatenate([v * beta, kb * eg], axis=-1)
    P = -A
    span = 1
    level = 0
    while span < C:
        split = level < GDN_SOLVE_SPLIT
        p_parts = _split_bf16(P) if split else (P.astype(BF16),)
        x_parts = _split_bf16(X) if split else (X.astype(BF16),)
        PX = _dot_parts(p_parts, x_parts)
        span *= 2
        level += 1
        if span < C:
            PP = _dot_parts(p_parts, p_parts)
        yield
        X = X + PX
        if span < C:
            P = PP
    u = X[:, :GDN_DV]
    w = X[:, GDN_DV:]
    wq = jnp.concatenate([w, q * eg], axis=0).astype(BF16)
    egl = jnp.broadcast_to(jnp.exp(g_last), (SUBLANE, LANE))
    return u, wq, k_tail_t, attn.astype(BF16), egl


def _gdn_apply_chunk(r0, C, hp, vals, s_sc, z_ref, onorm_ref, o_ref):
    u, wq, k_tail_t, attn, egl = vals
    lanes = slice(hp * LANE, (hp + 1) * LANE)
    S = s_sc[hp]
    wqS = _dot(wq, S.astype(BF16))
    yield
    v_new = u - wqS[:C]
    v_new_b = v_new.astype(BF16)
    kv = _dot(k_tail_t, v_new_b)
    av = _dot(attn, v_new_b)
    yield
    s_sc[hp] = S * egl[0:1, 0:1] + kv
    o = wqS[C:] + av
    o = o * lax.rsqrt(jnp.mean(o * o, axis=-1, keepdims=True) + NORM_EPS) * onorm_ref[...]
    o = o * _silu(z_ref[0, pl.ds(r0, C), lanes].astype(F32))
    o_ref[0, pl.ds(r0, C), lanes] = o.astype(o_ref.dtype)


def _gdn_core_kernel(q_ref, k_ref, v_ref, wq_ref, wk_ref, wv_ref, pq_ref, pk_ref, pv_ref, ab_ref, z_ref,
                     alog_ref, dtb_ref, onorm_ref, s0_ref, o_ref, sout_ref,
                     win_sc, s_sc, u_sc, wq_sc, kt_sc, at_sc, egl_sc, *, n_chunks, n_hp):
    C = GDN_CHUNK
    lane = lax.broadcasted_iota(jnp.int32, (1, LANE), 1)
    consts = []
    for hp in range(n_hp):
        head = pl.program_id(1) * n_hp + hp
        sel_a = lane == head
        consts.append((sel_a, lane == head + GDN_HEADS, -jnp.exp(_head_scalar(alog_ref, sel_a)),
                       _head_scalar(dtb_ref, sel_a)))
    refs = (q_ref, k_ref, v_ref, wq_ref, wk_ref, wv_ref, pq_ref, pk_ref, pv_ref, ab_ref, win_sc)
    tail = (s_sc, z_ref, onorm_ref, o_ref)

    def save(hp, vals):
        u_sc[hp], wq_sc[hp], kt_sc[hp], at_sc[hp], egl_sc[hp] = vals

    def load(hp):
        return u_sc[hp], wq_sc[hp], kt_sc[hp], at_sc[hp], egl_sc[hp]

    def row(c):
        r0 = N_META + c * C
        return r0 if isinstance(c, int) else pl.multiple_of(r0, BF16_ROWS)

    heads = range(n_hp)

    def prepare(r0, c_len, first):
        return [_gdn_prepare_chunk(r0, c_len, first, hp, refs, consts) for hp in heads]

    def apply(r0, c_len, vals):
        return [_gdn_apply_chunk(r0, c_len, hp, vals[hp], *tail) for hp in heads]

    s_sc[...] = s0_ref[0]
    _interleave(apply(0, N_META, _interleave(prepare(0, N_META, True))))
    for hp, vals in enumerate(_interleave(prepare(N_META, C, False))):
        save(hp, vals)

    def body(c, carry):
        vals = [load(hp) for hp in heads]
        out = _interleave(apply(row(c), C, vals) + prepare(row(c + 1), C, False))
        for hp in heads:
            save(hp, out[n_hp + hp])
        return carry

    lax.fori_loop(0, n_chunks - 1, body, 0)
    _interleave(apply(row(n_chunks - 1), C, [load(hp) for hp in heads]))
    sout_ref[0] = s_sc[...]


def _gdn_core(qkv, conv_w, conv_prev, ab, z, a_log, dt_bias, out_norm, S0):
    B, L, _ = qkv.shape
    n_chunks = (L - N_META) // GDN_CHUNK
    assert n_chunks >= 1 and N_META + n_chunks * GDN_CHUNK == L
    n_hp = 4
    ng = GDN_HEADS // n_hp
    wd = n_hp * LANE
    C = GDN_CHUNK
    col = lambda off: pl.BlockSpec((1, L, wd), lambda b, h: (b, 0, off + h))
    wcol = lambda off: pl.BlockSpec((GDN_CONV, wd), lambda b, h: (0, off + h))
    pcol = lambda off: pl.BlockSpec((1, GDN_CONV - 1, wd), lambda b, h: (b, 0, off + h))
    state = pl.BlockSpec((1, n_hp, GDN_DK, GDN_DV), lambda b, h: (b, h, 0, 0))
    return pl.pallas_call(
        functools.partial(_gdn_core_kernel, n_chunks=n_chunks, n_hp=n_hp),
        grid=(B, ng),
        in_specs=[col(0), col(ng), col(2 * ng), wcol(0), wcol(ng), wcol(2 * ng), pcol(0), pcol(ng), pcol(2 * ng),
                  pl.BlockSpec((1, L, LANE), lambda b, h: (b, 0, 0)), col(0),
                  _full((1, LANE)), _full((1, LANE)), _full((1, GDN_DV)), state],
        out_specs=[col(0), state],
        out_shape=[jax.ShapeDtypeStruct((B, L, GDN_VAL), BF16),
                   jax.ShapeDtypeStruct((B, GDN_HEADS, GDN_DK, GDN_DV), F32)],
        scratch_shapes=[pltpu.VMEM((3 * n_hp, BF16_ROWS + N_META, LANE), F32),
                        pltpu.VMEM((n_hp, GDN_DK, GDN_DV), F32),
                        pltpu.VMEM((n_hp, C, GDN_DV), F32), pltpu.VMEM((n_hp, 2 * C, GDN_DK), BF16),
                        pltpu.VMEM((n_hp, GDN_DK, C), BF16), pltpu.VMEM((n_hp, C, C), BF16),
                        pltpu.VMEM((n_hp, SUBLANE, LANE), F32)],
        compiler_params=_cparams(("parallel", "parallel")),
        name="gdn_core",
    )(qkv, qkv, qkv, conv_w, conv_w, conv_w, conv_prev, conv_prev, conv_prev, ab, z, a_log, dt_bias, out_norm, S0)


def _gdn_step_kernel(new_ref, prev_ref, w_ref, ab_ref, z_ref, alog_ref, dtb_ref, onorm_ref, s_ref, o_ref, sout_ref):
    w = w_ref[...]
    prev = prev_ref[0]
    y = new_ref[0] * w[GDN_CONV - 1:GDN_CONV]
    for j in range(GDN_CONV - 1):
        y = y + prev[j:j + 1] * w[j:j + 1]
    y = _silu(y)
    ab = ab_ref[0]
    z = z_ref[0]
    alog = alog_ref[...]
    dtb = dtb_ref[...]
    ri = lax.broadcasted_iota(jnp.int32, (GDN_DK, GDN_DK), 0)
    ci = lax.broadcasted_iota(jnp.int32, (GDN_DK, GDN_DK), 1)
    eye = ri == ci
    row8 = lax.broadcasted_iota(jnp.int32, (SUBLANE, GDN_DK), 0)

    def l2n(t):
        return t * lax.rsqrt(jnp.sum(t * t, axis=-1, keepdims=True) + NORM_EPS)

    for h in range(GDN_HEADS):
        q = l2n(y[:, h * GDN_DK:(h + 1) * GDN_DK]) * (GDN_DK ** -0.5)
        k = l2n(y[:, GDN_KEY + h * GDN_DK:GDN_KEY + (h + 1) * GDN_DK])
        v = y[:, 2 * GDN_KEY + h * GDN_DV:2 * GDN_KEY + (h + 1) * GDN_DV]
        g = -jnp.exp(alog[:, h:h + 1]) * _softplus(ab[:, h:h + 1] + dtb[:, h:h + 1])
        beta = _sigmoid(ab[:, GDN_HEADS + h:GDN_HEADS + h + 1])
        eg = jnp.exp(g)
        S = s_ref[0, h]
        kq = jnp.where(row8 == 0, jnp.broadcast_to(k, (SUBLANE, GDN_DK)),
                       jnp.where(row8 == 1, jnp.broadcast_to(q, (SUBLANE, GDN_DK)), 0.0))
        kqS = _dot_hi(kq, S)
        v_new = beta * (v - eg * kqS[0:1])
        qk = jnp.sum(q * k, axis=-1, keepdims=True)
        o = eg * kqS[1:2] + qk * v_new
        k_col = jnp.sum(jnp.where(eye, jnp.broadcast_to(k, (GDN_DK, GDN_DK)), 0.0), axis=1, keepdims=True)
        sout_ref[0, h] = S * eg + k_col * v_new
        o = o * lax.rsqrt(jnp.mean(o * o, axis=-1, keepdims=True) + NORM_EPS) * onorm_ref[...]
        o = o * _silu(z[:, h * GDN_DV:(h + 1) * GDN_DV])
        o_ref[0, :, h * GDN_DV:(h + 1) * GDN_DV] = o.astype(o_ref.dtype)


def _gdn_step(qkv_new, conv_prev, conv_w, ab, z, a_log, dt_bias, out_norm, S0):
    Bs = qkv_new.shape[0]
    per = lambda *s: pl.BlockSpec((1,) + s, lambda b: (b,) + (0,) * len(s))
    return pl.pallas_call(
        _gdn_step_kernel,
        grid=(Bs,),
        in_specs=[per(1, GDN_QKV), per(GDN_CONV - 1, GDN_QKV), _full((GDN_CONV, GDN_QKV)), per(1, LANE),
                  per(1, GDN_VAL), _full((1, LANE)), _full((1, LANE)), _full((1, GDN_DV)),
                  per(GDN_HEADS, GDN_DK, GDN_DV)],
        out_specs=[per(1, GDN_VAL), per(GDN_HEADS, GDN_DK, GDN_DV)],
        out_shape=[jax.ShapeDtypeStruct((Bs, 1, GDN_VAL), BF16),
                   jax.ShapeDtypeStruct((Bs, GDN_HEADS, GDN_DK, GDN_DV), F32)],
        compiler_params=_cparams(("parallel",)),
        name="gdn_step",
    )(qkv_new, conv_prev, conv_w, ab, z, a_log, dt_bias, out_norm, S0)


def _proj_residual_kernel(x_ref, o_ref, w_ref, g_ref, out_ref):
    y = _dot(o_ref[0], w_ref[...])
    out_ref[0] = x_ref[0] + _rms(y, g_ref[...])


def _proj_residual(x, o, w, g):
    B, L, D = x.shape
    K = o.shape[-1]
    tl = _row_tile(L, 700, BF16_ROWS)
    return pl.pallas_call(
        _proj_residual_kernel,
        grid=(B, L // tl),
        in_specs=[pl.BlockSpec((1, tl, D), lambda b, l: (b, l, 0)), pl.BlockSpec((1, tl, K), lambda b, l: (b, l, 0)),
                  _full((K, D)), _full((1, D))],
        out_specs=pl.BlockSpec((1, tl, D), lambda b, l: (b, l, 0)),
        out_shape=jax.ShapeDtypeStruct((B, L, D), F32),
        compiler_params=_cparams(("parallel", "parallel")),
        name="proj_residual",
    )(x, o, w, g)


def _ffn_act(j, u, tap1, tap0, cw_ref, cb_ref):
    cw = cw_ref[j]
    conv = u * cw[2:3] + tap1 * cw[1:2] + tap0 * cw[0:1] + cb_ref[j]
    return (_silu(conv[:, :FFN_CHUNK]) * conv[:, FFN_CHUNK:]).astype(BF16)


def _ffn_seq_kernel(x_ref, o_ref, wo_ref, go_ref, gpre_ref, wup_ref, cw_ref, cb_ref, wdn_ref, gpost_ref, prev_ref,
                    out_ref, newprev_ref, h_sc, acc_sc, a_sc, u_sc, carry_sc, *, tl):
    x = x_ref[0] + _rms(_dot(o_ref[0], wo_ref[...]), go_ref[...])
    out_ref[0] = x
    h_sc[...] = _rms(x, gpre_ref[...]).astype(BF16)
    acc_sc[...] = jnp.zeros_like(acc_sc)

    @pl.when(pl.program_id(1) == 0)
    def _():
        carry_sc[...] = jnp.zeros_like(carry_sc)
        carry_sc[:, SUBLANE - (FFN_CONV - 1):SUBLANE, :] = prev_ref[0]

    def up(j):
        return _dot(h_sc[...], wup_ref[j])

    def put(slot, j, u):
        u_sc[slot, 0:SUBLANE, :] = carry_sc[j]
        u_sc[slot, SUBLANE:SUBLANE + tl, :] = u
        carry_sc[j] = u[tl - SUBLANE:tl]
        newprev_ref[0, j] = u[tl - (FFN_CONV - 1):tl]

    def act_down(slot, j):
        a_sc[slot] = _ffn_act(j, u_sc[slot, SUBLANE:SUBLANE + tl, :], u_sc[slot, SUBLANE - 1:SUBLANE - 1 + tl, :],
                              u_sc[slot, SUBLANE - 2:SUBLANE - 2 + tl, :], cw_ref, cb_ref)
        acc_sc[...] += _dot(a_sc[slot], wdn_ref[j])

    def step(slot, j):
        u_next = up(j + 1)
        act_down(slot, j)
        put(1 - slot, j + 1, u_next)

    put(0, 0, up(0))

    def body(jj, carry):
        step(0, 2 * jj)
        step(1, 2 * jj + 1)
        return carry

    n_pairs = (FFN_NCH - 1) // 2
    lax.fori_loop(0, n_pairs, body, 0)
    for j in range(2 * n_pairs, FFN_NCH - 1):
        step(j % 2, j)
    act_down((FFN_NCH - 1) % 2, FFN_NCH - 1)
    out_ref[0] += _rms(acc_sc[...], gpost_ref[...])


def _ffn_step_kernel(x_ref, gpre_ref, wup_ref, cw_ref, cb_ref, wdn_ref, gpost_ref, prev_ref, out_ref, u_ref,
                     h_sc, acc_sc):
    x = x_ref[0]
    h_sc[...] = _rms(x, gpre_ref[...]).astype(BF16)
    acc_sc[...] = jnp.zeros_like(acc_sc)

    def body(j, carry):
        u = _dot(h_sc[...], wup_ref[j])
        u_ref[j] = u
        acc_sc[...] += _dot(_ffn_act(j, u, prev_ref[j, 1], prev_ref[j, 0], cw_ref, cb_ref), wdn_ref[j])
        return carry

    lax.fori_loop(0, FFN_NCH, body, 0)
    out_ref[0] = x + _rms(acc_sc[...], gpost_ref[...])


def _ffn_weight_specs():
    w2 = 2 * FFN_CHUNK
    once = pl.Buffered(1)
    return [_full((1, D_MODEL)), _full((FFN_NCH, D_MODEL, w2), once), _full((FFN_NCH, FFN_CONV, w2)),
            _full((FFN_NCH, 1, w2)), _full((FFN_NCH, FFN_CHUNK, D_MODEL), once), _full((1, D_MODEL))]


def _chunk_cols(t):
    lead = t.shape[:-1]
    t = t.reshape(lead + (2, FFN_NCH, FFN_CHUNK))
    t = jnp.moveaxis(t, -2, 0)
    return t.reshape((FFN_NCH,) + lead + (2 * FFN_CHUNK,))


def _unchunk_cols(t):
    lead = t.shape[1:-1]
    t = t.reshape((FFN_NCH,) + lead + (2, FFN_CHUNK))
    t = jnp.moveaxis(t, 0, -2)
    return t.reshape(lead + (2 * D_FF,))


def _ffn_seq(x, o, w_o, g_o, prev, wts):
    B, L, D = x.shape
    K = o.shape[-1]
    tl = _row_tile(L, 700, BF16_ROWS)
    w2 = 2 * FFN_CHUNK
    prev_c = jnp.moveaxis(_chunk_cols(prev), 0, 1)
    out, newprev = pl.pallas_call(
        functools.partial(_ffn_seq_kernel, tl=tl),
        grid=(B, L // tl),
        in_specs=[pl.BlockSpec((1, tl, D), lambda b, l: (b, l, 0)), pl.BlockSpec((1, tl, K), lambda b, l: (b, l, 0)),
                  _full((K, D), pl.Buffered(1)), _full((1, D))] + _ffn_weight_specs()
        + [pl.BlockSpec((1, FFN_NCH, FFN_CONV - 1, w2), lambda b, l: (b, 0, 0, 0))],
        out_specs=[pl.BlockSpec((1, tl, D), lambda b, l: (b, l, 0)),
                   pl.BlockSpec((1, FFN_NCH, FFN_CONV - 1, w2), lambda b, l: (b, 0, 0, 0))],
        out_shape=[jax.ShapeDtypeStruct((B, L, D), F32), jax.ShapeDtypeStruct((B, FFN_NCH, FFN_CONV - 1, w2), F32)],
        scratch_shapes=[pltpu.VMEM((tl, D), BF16), pltpu.VMEM((tl, D), F32), pltpu.VMEM((2, tl, FFN_CHUNK), BF16),
                        pltpu.VMEM((2, SUBLANE + tl, w2), F32),
                        pltpu.VMEM((FFN_NCH, SUBLANE, w2), F32)],
        compiler_params=_cparams(("parallel", "arbitrary")),
        name="ffn_seq",
    )(x, o, w_o, g_o, *wts, prev_c)
    return out, _unchunk_cols(jnp.moveaxis(newprev, 1, 0))


def _ffn_step(x, prev, wts):
    _, Bs, D = x.shape
    w2 = 2 * FFN_CHUNK
    prev_c = _chunk_cols(jnp.moveaxis(prev, 1, 0))
    out, u = pl.pallas_call(
        _ffn_step_kernel,
        grid=(1,),
        in_specs=[_full((1, Bs, D))] + _ffn_weight_specs() + [_full((FFN_NCH, FFN_CONV - 1, Bs, w2))],
        out_specs=[_full((1, Bs, D)), _full((FFN_NCH, Bs, w2))],
        out_shape=[jax.ShapeDtypeStruct((1, Bs, D), F32), jax.ShapeDtypeStruct((FFN_NCH, Bs, w2), F32)],
        scratch_shapes=[pltpu.VMEM((Bs, D), BF16), pltpu.VMEM((Bs, D), F32)],
        compiler_params=_cparams(("arbitrary",)),
        name="ffn_step",
    )(x, *wts, prev_c)
    return out, jnp.concatenate([prev[:, 1:], _unchunk_cols(u)[:, None]], axis=1)


def _rot_half(t, width):
    lane = lax.broadcasted_iota(jnp.int32, t.shape, t.ndim - 1)
    first = (lane % MLA_ROPE) < (MLA_ROPE // 2)
    return jnp.where(first, pltpu.roll(t, width - MLA_ROPE // 2, t.ndim - 1), pltpu.roll(t, MLA_ROPE // 2, t.ndim - 1))


def _mla_proj_kernel(x_ref, gkv_ref, wkva_ref, gkva_ref, gq_ref, wqa_ref, gqa_ref, wqb_ref, wuk_ref, cs_ref,
                     c_ref, kr_ref, kcat_ref, q_ref):
    x = x_ref[0]
    cos = cs_ref[:, 0:LANE]
    sin = cs_ref[:, LANE:2 * LANE]
    lane = lax.broadcasted_iota(jnp.int32, (1, LANE), 1)

    ckv = _dot(_rms(x, gkv_ref[...]).astype(BF16), wkva_ref[...])
    c = _rms(ckv[:, :MLA_KV_LORA], gkva_ref[...])
    r = ckv[:, MLA_KV_LORA:MLA_QK]
    kr = r * cos + _rot_half(r, LANE) * sin
    c_ref[0] = c
    kr_ref[0] = kr[:, :MLA_ROPE]
    kcat_ref[0, :, 0:MLA_KV_LORA] = c.astype(BF16)
    kcat_ref[0, :, MLA_KV_LORA:MLA_QK] = kr.astype(BF16)

    qa = _dot(_rms(x, gq_ref[...]).astype(BF16), wqa_ref[...])
    qb = _dot(_rms(qa, gqa_ref[...]).astype(BF16), wqb_ref[...])
    n_nope = MLA_HEADS * MLA_NOPE
    n_pe = MLA_HEADS * MLA_ROPE
    pe = qb[:, n_nope:n_nope + n_pe]
    reps = n_pe // LANE
    pe = pe * jnp.concatenate([cos] * reps, axis=1) + _rot_half(pe, n_pe) * jnp.concatenate([sin] * reps, axis=1)
    for h in range(MLA_HEADS):
        q_lat = _dot(qb[:, h * MLA_NOPE:(h + 1) * MLA_NOPE].astype(BF16), wuk_ref[h])
        q_ref[0, h, :, 0:MLA_KV_LORA] = (q_lat * MLA_SCALE).astype(BF16)
        t = pe[:, (h // 2) * LANE:(h // 2 + 1) * LANE]
        if h % 2 == 1:
            t = pltpu.roll(t, MLA_ROPE, 1)
        q_ref[0, h, :, MLA_KV_LORA:MLA_QK] = jnp.where(lane < MLA_ROPE, t * MLA_SCALE, 0.0).astype(BF16)


def _mla_proj(x, cs, gkv, wkva, gkva, gq, wqa, gqa, wqb, wuk):
    B, L, D = x.shape
    tl = _row_tile(L, 700, BF16_ROWS)
    row = lambda n: pl.BlockSpec((1, tl, n), lambda b, l: (b, l, 0))
    return pl.pallas_call(
        _mla_proj_kernel,
        grid=(B, L // tl),
        in_specs=[row(D), _full((1, D)), _full((D, MLA_QK)), _full((1, MLA_KV_LORA)), _full((1, D)),
                  _full((D, MLA_Q_LORA)), _full((1, MLA_Q_LORA)), _full(wqb.shape), _full(wuk.shape),
                  pl.BlockSpec((tl, 2 * LANE), lambda b, l: (l, 0))],
        out_specs=[row(MLA_KV_LORA), row(MLA_ROPE), row(MLA_QK),
                   pl.BlockSpec((1, MLA_HEADS, tl, MLA_QK), lambda b, l: (b, 0, l, 0))],
        out_shape=[jax.ShapeDtypeStruct((B, L, MLA_KV_LORA), F32), jax.ShapeDtypeStruct((B, L, MLA_ROPE), F32),
                   jax.ShapeDtypeStruct((B, L, MLA_QK), BF16),
                   jax.ShapeDtypeStruct((B, MLA_HEADS, L, MLA_QK), BF16)],
        compiler_params=_cparams(("parallel", "parallel")),
        name="mla_proj",
    )(x, gkv, wkva, gkva, gq, wqa, gqa, wqb, wuk, cs)


def _attn_kernel(q_ref, kcat_ref, wuv_ref, o_ref, m_sc, l_sc, acc_sc, *, tq, n_hp):
    i = pl.program_id(1)
    m_sc[...] = jnp.full_like(m_sc, -jnp.inf)
    l_sc[...] = jnp.zeros_like(l_sc)
    acc_sc[...] = jnp.zeros_like(acc_sc)
    t_a = -(-(tq // 2) // BF16_ROWS) * BF16_ROWS

    def block(hp, r0, nr, kt, nk, masked):
        rows = slice(r0, r0 + nr)
        kv = kcat_ref[0, pl.ds(pl.multiple_of(kt * tq, BF16_ROWS), nk), :]
        s = _dot_nt(q_ref[0, hp, rows, :], kv)
        yield
        if masked:
            ri = lax.broadcasted_iota(jnp.int32, (nr, nk), 0)
            ci = lax.broadcasted_iota(jnp.int32, (nr, nk), 1)
            s = jnp.where(ci <= ri + r0, s, -jnp.inf)
        m_prev = m_sc[hp, rows]
        m_new = jnp.maximum(m_prev, jnp.max(s, axis=1, keepdims=True))
        alpha = jnp.exp(m_prev - m_new)
        p = jnp.exp(s - m_new)
        l_sc[hp, rows] = alpha * l_sc[hp, rows] + jnp.sum(p, axis=1, keepdims=True)
        m_sc[hp, rows] = m_new
        pv = _dot(p.astype(BF16), kv[:, :MLA_KV_LORA])
        yield
        acc_sc[hp, rows] = alpha * acc_sc[hp, rows] + pv

    def body(kt, carry):
        _interleave([block(hp, 0, tq, kt, tq, False) for hp in range(n_hp)])
        return carry

    lax.fori_loop(0, i, body, 0)
    diagonal = []
    for hp in range(n_hp):
        diagonal.append(block(hp, 0, t_a, i, t_a, True))
        if t_a < tq:
            diagonal.append(block(hp, t_a, tq - t_a, i, tq, True))
    _interleave(diagonal)
    for hp in range(n_hp):
        o_lat = (acc_sc[hp] / l_sc[hp]).astype(BF16)
        o_ref[0, :, hp * MLA_V:(hp + 1) * MLA_V] = _dot(o_lat, wuv_ref[hp]).astype(o_ref.dtype)


def _attn_seq(q, kcat, wuv):
    B, nh, L, _ = q.shape
    tq = _row_tile(L, 700, BF16_ROWS)
    n_hp = 2
    return pl.pallas_call(
        functools.partial(_attn_kernel, tq=tq, n_hp=n_hp),
        grid=(B, L // tq, nh // n_hp),
        in_specs=[pl.BlockSpec((1, n_hp, tq, MLA_QK), lambda b, i, h: (b, h, i, 0)),
                  pl.BlockSpec((1, L, MLA_QK), lambda b, i, h: (b, 0, 0)),
                  pl.BlockSpec((n_hp, MLA_KV_LORA, MLA_V), lambda b, i, h: (h, 0, 0))],
        out_specs=pl.BlockSpec((1, tq, n_hp * MLA_V), lambda b, i, h: (b, i, h)),
        out_shape=jax.ShapeDtypeStruct((B, L, nh * MLA_V), BF16),
        scratch_shapes=[pltpu.VMEM((n_hp, tq, 1), F32), pltpu.VMEM((n_hp, tq, 1), F32),
                        pltpu.VMEM((n_hp, tq, MLA_KV_LORA), F32)],
        compiler_params=_cparams(("parallel", "parallel", "parallel")),
        name="attn_seq",
    )(q, kcat, wuv)


def _attn_paged_kernel(pt_ref, q_ref, kcat_ref, cache_c_ref, cache_k_ref, o_ref, cbuf, kbuf, c16, s_sc, sem,
                       *, n_pages, page, kchunk):
    b = pl.program_id(0)
    nb = pl.num_programs(0)
    P = n_pages * page

    def copies(bb, slot):
        out = []
        for pg in range(n_pages):
            idx = pt_ref[bb, pg]
            out.append(pltpu.make_async_copy(cache_c_ref.at[idx], cbuf.at[slot, pl.ds(pg * page, page), :],
                                             sem.at[0, slot]))
            out.append(pltpu.make_async_copy(cache_k_ref.at[idx], kbuf.at[slot, :, pl.ds(pg * page, page)],
                                             sem.at[1, slot]))
        return out

    @pl.when(b == 0)
    def _():
        for cp in copies(0, 0):
            cp.start()

    @pl.when(b + 1 < nb)
    def _():
        for cp in copies(b + 1, (b + 1) % 2):
            cp.start()

    slot = b % 2
    for cp in copies(b, slot):
        cp.wait()

    q = q_ref[0]
    q_lat = q[:, :MLA_KV_LORA]
    q_pe = q[:, MLA_KV_LORA:MLA_KV_LORA + MLA_ROPE]
    for c0 in range(0, P, kchunk):
        cc = cbuf[slot, c0:c0 + kchunk, :].astype(BF16)
        c16[c0:c0 + kchunk, :] = cc
        kk = kbuf[slot, :, c0:c0 + kchunk].astype(BF16)
        s_sc[:, c0:c0 + kchunk] = _dot_nt(q_lat, cc) + _dot(q_pe, kk)
    kself = kcat_ref[0]
    s_self = jnp.sum(q.astype(F32) * kself.astype(F32), axis=-1, keepdims=True)
    s = s_sc[...]
    m = jnp.maximum(jnp.max(s, axis=-1, keepdims=True), s_self)
    p = jnp.exp(s - m)
    p_self = jnp.exp(s_self - m)
    denom = jnp.sum(p, axis=-1, keepdims=True) + p_self
    s_sc[...] = p
    acc = p_self * kself[:, :MLA_KV_LORA].astype(F32)
    for c0 in range(0, P, kchunk):
        acc = acc + _dot(s_sc[:, c0:c0 + kchunk].astype(BF16), c16[c0:c0 + kchunk, :])
    o_ref[0] = (acc / denom).astype(o_ref.dtype)


def _attn_paged(q, kcat, cache_c, cache_k, page_table):
    Bs, nh, _ = q.shape
    n_pages = page_table.shape[1]
    page = cache_c.shape[1]
    cache_k = jnp.swapaxes(cache_k, 1, 2)
    P = n_pages * page
    kchunk = _row_tile(P, 1024, LANE)
    grid_spec = pltpu.PrefetchScalarGridSpec(
        num_scalar_prefetch=1,
        grid=(Bs,),
        in_specs=[pl.BlockSpec((1, nh, MLA_QK), lambda b, pt: (b, 0, 0)),
                  pl.BlockSpec((1, 1, MLA_QK), lambda b, pt: (b, 0, 0)),
                  pl.BlockSpec(memory_space=pl.ANY), pl.BlockSpec(memory_space=pl.ANY)],
        out_specs=pl.BlockSpec((1, nh, MLA_KV_LORA), lambda b, pt: (b, 0, 0)),
        scratch_shapes=[pltpu.VMEM((2, P, MLA_KV_LORA), F32), pltpu.VMEM((2, MLA_ROPE, P), F32),
                        pltpu.VMEM((P, MLA_KV_LORA), BF16), pltpu.VMEM((nh, P), F32),
                        pltpu.SemaphoreType.DMA((2, 2))],
    )
    return pl.pallas_call(
        functools.partial(_attn_paged_kernel, n_pages=n_pages, page=page, kchunk=kchunk),
        grid_spec=grid_spec,
        out_shape=jax.ShapeDtypeStruct((Bs, nh, MLA_KV_LORA), BF16),
        compiler_params=_cparams(("arbitrary",)),
        name="attn_paged",
    )(page_table, q, kcat, cache_c, cache_k)


def _mla_out_kernel(x_ref, o_ref, wuv_ref, wout_ref, g_ref, out_ref, acc_sc):
    for h in range(MLA_HEADS):
        oh = _dot(o_ref[0, h], wuv_ref[h]).astype(BF16)
        y = _dot(oh, wout_ref[h])
        if h == 0:
            acc_sc[...] = y
        else:
            acc_sc[...] += y
    out_ref[0] = x_ref[0] + _rms(acc_sc[...], g_ref[...])


def _mla_out(x, o_lat, wuv, wout, g):
    B, L, D = x.shape
    tl = _row_tile(L, 700, BF16_ROWS)
    return pl.pallas_call(
        _mla_out_kernel,
        grid=(B, L // tl),
        in_specs=[pl.BlockSpec((1, tl, D), lambda b, l: (b, l, 0)),
                  pl.BlockSpec((1, MLA_HEADS, tl, MLA_KV_LORA), lambda b, l: (b, 0, l, 0)),
                  _full(wuv.shape), _full(wout.shape), _full((1, D))],
        out_specs=pl.BlockSpec((1, tl, D), lambda b, l: (b, l, 0)),
        out_shape=jax.ShapeDtypeStruct((B, L, D), F32),
        scratch_shapes=[pltpu.VMEM((tl, D), F32)],
        compiler_params=_cparams(("parallel", "parallel")),
        name="mla_out",
    )(x, o_lat, wuv, wout, g)


def _rope_table(positions):
    half = MLA_ROPE // 2
    inv = ROPE_THETA ** (-jnp.arange(half, dtype=F32) / half)
    ang = positions.astype(F32)[:, None] * inv[None, :]
    cos, sin = jnp.cos(ang), jnp.sin(ang)
    reps = LANE // MLA_ROPE
    return jnp.concatenate([jnp.concatenate([cos, cos] * reps, axis=1), jnp.concatenate([-sin, sin] * reps, axis=1)],
                           axis=1)


def _row(v, width=None):
    v = v.reshape(1, -1).astype(F32)
    if width is not None and v.shape[1] < width:
        v = jnp.pad(v, ((0, 0), (0, width - v.shape[1])))
    return v


def _prepare(p):
    i = 0
    w_in = p['a_w_in'][i]
    n_ab = 2 * GDN_HEADS
    wq_b = p['b_w_q_b'][i].reshape(MLA_Q_LORA, MLA_HEADS, MLA_NOPE + MLA_ROPE)
    wq_b = jnp.concatenate([wq_b[:, :, :MLA_NOPE].reshape(MLA_Q_LORA, -1), wq_b[:, :, MLA_NOPE:].reshape(MLA_Q_LORA, -1)],
                           axis=1)
    prep = {
        'a_norm_pre': _row(p['a_norm_pre'][i]), 'a_norm_post': _row(p['a_norm_post'][i]),
        'a_wqkv': w_in[:, :GDN_QKV].astype(BF16), 'a_wz': w_in[:, GDN_QKV:GDN_QKV + GDN_VAL].astype(BF16),
        'a_wab': jnp.pad(w_in[:, GDN_QKV + GDN_VAL:], ((0, 0), (0, LANE - n_ab))).astype(BF16),
        'a_conv_w': p['a_conv_w'][i].astype(F32), 'a_log': _row(p['a_log'][i], LANE),
        'a_dt_bias': _row(p['a_dt_bias'][i], LANE), 'a_out_norm': _row(p['a_out_norm'][i]),
        'a_w_out': p['a_w_out'][i].astype(BF16),
        'kv_norm': _row(p['kv_norm']),
        'kv_w_a': jnp.pad(p['kv_w_a'], ((0, 0), (0, MLA_QK - MLA_KV_LORA - MLA_ROPE))).astype(BF16),
        'kv_a_norm': _row(p['kv_a_norm']),
        'kv_w_uk': jnp.transpose(p['kv_w_uk'], (1, 2, 0)).astype(BF16),
        'kv_w_uv': jnp.transpose(p['kv_w_uv'], (1, 0, 2)).astype(BF16),
        'b_norm_pre': _row(p['b_norm_pre'][i]), 'b_norm_post': _row(p['b_norm_post'][i]),
        'b_w_q_a': p['b_w_q_a'][i].astype(BF16), 'b_q_a_norm': _row(p['b_q_a_norm'][i]),
        'b_w_q_b': wq_b.astype(BF16),
        'b_w_out': p['b_w_out'][i].reshape(MLA_HEADS, MLA_V, D_MODEL).astype(BF16),
        'ffn': [],
    }
    for layer in range(2):
        w_up = jnp.moveaxis(_chunk_cols(p['f_w_up'][layer]), 0, 0).astype(BF16)
        prep['ffn'].append((
            _row(p['f_norm_pre'][layer]), w_up, _chunk_cols(p['f_conv_w'][layer]).astype(F32),
            _chunk_cols(p['f_conv_b'][layer].reshape(1, -1)).astype(F32),
            p['f_w_down'][layer].reshape(FFN_NCH, FFN_CHUNK, D_MODEL).astype(BF16), _row(p['f_norm_post'][layer])))
    return prep


def _mla_block(x, cs, prep):
    return _mla_proj(x, cs, prep['kv_norm'], prep['kv_w_a'], prep['kv_a_norm'], prep['b_norm_pre'], prep['b_w_q_a'],
                     prep['b_q_a_norm'], prep['b_w_q_b'], prep['kv_w_uk'])


def _trunk_seq(x, S0, dconv0, fconv0, prep):
    B, L, _ = x.shape
    assert L >= SUBLANE
    qkv, z, ab, tail = _gdn_in(x, prep['a_norm_pre'], prep['a_wqkv'], prep['a_wz'], prep['a_wab'], BF16)
    o, S = _gdn_core(qkv, prep['a_conv_w'], dconv0[0], ab, z, prep['a_log'], prep['a_dt_bias'], prep['a_out_norm'], S0[0])
    dconv = tail[:, SUBLANE - (GDN_CONV - 1):]
    x, fconv_a = _ffn_seq(x, o, prep['a_w_out'], prep['a_norm_post'], fconv0[0], prep['ffn'][0])
    cs = _rope_table(jnp.arange(L, dtype=jnp.int32))
    c, kr, kcat, q = _mla_block(x, cs, prep)
    o = _attn_seq(q, kcat, prep['kv_w_uv'])
    x, fconv_b = _ffn_seq(x, o, prep['b_w_out'].reshape(MLA_HEADS * MLA_V, D_MODEL), prep['b_norm_post'], fconv0[1],
                          prep['ffn'][1])
    return x, S[None], dconv[None], jnp.stack([fconv_a, fconv_b]), c, kr


def _trunk_step(x, pos, S0, dconv0, fconv0, cache_c, cache_k, page_table, prep):
    Bs = x.shape[0]
    xt = x.reshape(1, Bs, D_MODEL)
    qkv, z, ab, _ = _gdn_in(xt, prep['a_norm_pre'], prep['a_wqkv'], prep['a_wz'], prep['a_wab'], F32)
    o, S = _gdn_step(qkv.reshape(Bs, 1, -1), dconv0[0], prep['a_conv_w'], ab.reshape(Bs, 1, -1), z.reshape(Bs, 1, -1),
                     prep['a_log'], prep['a_dt_bias'], prep['a_out_norm'], S0[0])
    dconv = jnp.concatenate([dconv0[0][:, 1:], qkv.reshape(Bs, 1, -1)], axis=1)
    xt = _proj_residual(xt, o.reshape(1, Bs, -1), prep['a_w_out'], prep['a_norm_post'])
    xt, fconv_a = _ffn_step(xt, fconv0[0], prep['ffn'][0])
    cs = _rope_table(jnp.full((Bs,), pos, jnp.int32))
    c, kr, kcat, q = _mla_block(xt, cs, prep)
    q = jnp.moveaxis(q[0], 0, 1)
    o_lat = _attn_paged(q, kcat.reshape(Bs, 1, MLA_QK), cache_c, cache_k, page_table)
    o_lat = jnp.moveaxis(o_lat, 0, 1)[None]
    xt = _mla_out(xt, o_lat, prep['kv_w_uv'], prep['b_w_out'], prep['b_norm_post'])
    xt, fconv_b = _ffn_step(xt, fconv0[1], prep['ffn'][1])
    return (xt.reshape(Bs, 1, D_MODEL), S[None], dconv[None], jnp.stack([fconv_a, fconv_b]),
            c.reshape(Bs, 1, MLA_KV_LORA), kr.reshape(Bs, 1, MLA_ROPE))


def kernel(x_prompt, x_sample, state_delta_S, state_delta_conv, state_ffn_conv, cache_kv_latent, cache_k_rope,
           page_table, meta_tokens, a_norm_pre, a_norm_post, a_w_in, a_conv_w, a_log, a_dt_bias, a_out_norm, a_w_out,
           kv_norm, kv_w_a, kv_a_norm, kv_w_uk, kv_w_uv, b_norm_pre, b_norm_post, b_w_q_a, b_q_a_norm, b_w_q_b,
           b_w_out, f_norm_pre, f_norm_post, f_w_up, f_conv_w, f_conv_b, f_w_down):
    prep = _prepare({
        'a_norm_pre': a_norm_pre, 'a_norm_post': a_norm_post, 'a_w_in': a_w_in, 'a_conv_w': a_conv_w, 'a_log': a_log,
        'a_dt_bias': a_dt_bias, 'a_out_norm': a_out_norm, 'a_w_out': a_w_out, 'kv_norm': kv_norm, 'kv_w_a': kv_w_a,
        'kv_a_norm': kv_a_norm, 'kv_w_uk': kv_w_uk, 'kv_w_uv': kv_w_uv, 'b_norm_pre': b_norm_pre,
        'b_norm_post': b_norm_post, 'b_w_q_a': b_w_q_a, 'b_q_a_norm': b_q_a_norm, 'b_w_q_b': b_w_q_b,
        'b_w_out': b_w_out, 'f_norm_pre': f_norm_pre, 'f_norm_post': f_norm_post, 'f_w_up': f_w_up,
        'f_conv_w': f_conv_w, 'f_conv_b': f_conv_b, 'f_w_down': f_w_down})

    bp = x_prompt.shape[0]
    xp = jnp.concatenate([jnp.broadcast_to(meta_tokens.astype(x_prompt.dtype)[None], (bp, N_META, D_MODEL)), x_prompt],
                         axis=1)
    zS = jnp.zeros((1, bp, GDN_HEADS, GDN_DK, GDN_DV), state_delta_S.dtype)
    zdc = jnp.zeros((1, bp, GDN_CONV - 1, GDN_QKV), x_prompt.dtype)
    zfc = jnp.zeros((2, bp, FFN_CONV - 1, 2 * D_FF), x_prompt.dtype)
    yp, p_S, p_dconv, p_fconv, p_c, p_kr = _trunk_seq(xp, zS, zdc, zfc, prep)

    past_len = page_table.shape[1] * cache_kv_latent.shape[1]
    ys, s_S, s_dconv, s_fconv, s_c, s_kr = _trunk_step(x_sample, past_len, state_delta_S, state_delta_conv,
                                                       state_ffn_conv, cache_kv_latent, cache_k_rope, page_table, prep)
    return (yp[:, N_META:], ys, p_S, p_dconv, p_fconv, p_c, p_kr, s_S, s_dconv, s_fconv, s_c, s_kr)
```

```python
import functools
import math

import jax
import jax.numpy as jnp
from jax import lax
from jax.experimental import pallas as pl
from jax.experimental.pallas import tpu as pltpu

F32 = jnp.float32
BF16 = jnp.bfloat16

D_MODEL = 1024
N_META = 16
NORM_EPS = 1e-6

GDN_HEADS = 8
GDN_DK = 128
GDN_DV = 128
GDN_CONV = 4
GDN_CHUNK = 64
GDN_KEY = GDN_HEADS * GDN_DK
GDN_VAL = GDN_HEADS * GDN_DV
GDN_QKV = 2 * GDN_KEY + GDN_VAL
GDN_SOLVE_SPLIT = 3

MLA_HEADS = 8
MLA_Q_LORA = 384
MLA_KV_LORA = 256
MLA_NOPE = 128
MLA_ROPE = 64
MLA_V = 128
MLA_SCALE = 1.0 / math.sqrt(MLA_NOPE + MLA_ROPE)
ROPE_THETA = 10000.0
MLA_QK = MLA_KV_LORA + 128
ATTN_ROW_BLOCKS = 2

D_FF = 2816
FFN_CONV = 3
FFN_CHUNK = 256
FFN_NCH = D_FF // FFN_CHUNK

LANE = 128
SUBLANE = 8
BF16_ROWS = 16
VMEM_LIMIT = 56 * 1024 * 1024

HI = lax.Precision.HIGHEST


def _row_tile(n, cap, align):
    best = None
    for t in range(align, min(n, cap) + 1, align):
        if n % t == 0:
            best = t
    return n if best is None else best


def _cparams(sem, flags=None):
    return pltpu.CompilerParams(dimension_semantics=sem, vmem_limit_bytes=VMEM_LIMIT, flags=flags)


def _rms(x, w):
    return x * lax.rsqrt(jnp.mean(x * x, axis=-1, keepdims=True) + NORM_EPS) * w


def _sigmoid(x):
    return 1.0 / (1.0 + jnp.exp(-x))


def _silu(x):
    return x * _sigmoid(x)


def _softplus(x):
    return jnp.maximum(x, 0.0) + jnp.log(1.0 + jnp.exp(-jnp.abs(x)))


def _dot(a, b):
    return jnp.dot(a, b, preferred_element_type=F32)


def _dot_nt(a, b, precision=None):
    return lax.dot_general(a, b, (((1,), (1,)), ((), ())), precision=precision, preferred_element_type=F32)


def _split_bf16(t):
    hi = t.astype(BF16)
    return hi, (t - hi.astype(F32)).astype(BF16)


def _dot_parts(a_parts, b_parts):
    out = _dot(a_parts[0], b_parts[0])
    if len(a_parts) > 1:
        out = out + (_dot(a_parts[0], b_parts[1]) + _dot(a_parts[1], b_parts[0]))
    return out


def _interleave(gens):
    results = [None] * len(gens)
    active = list(range(len(gens)))
    while active:
        for i in list(active):
            try:
                next(gens[i])
            except StopIteration as stop:
                results[i] = stop.value
                active.remove(i)
    return results


def _dot_hi(a, b):
    return jnp.dot(a, b, precision=HI, preferred_element_type=F32)


def _full(shape, pipeline_mode=None):
    nd = len(shape)
    return pl.BlockSpec(shape, lambda *_: (0,) * nd, pipeline_mode=pipeline_mode)


def _gdn_in_kernel(x_ref, g_ref, wqkv_ref, wz_ref, wab_ref, qkv_ref, z_ref, ab_ref, tail_ref, *, tl):
    h = _rms(x_ref[0], g_ref[...]).astype(BF16)
    for n0 in range(0, GDN_QKV, 512):
        r = _dot(h, wqkv_ref[:, n0:n0 + 512])
        qkv_ref[0, :, n0:n0 + 512] = r.astype(qkv_ref.dtype)
        tail_ref[0, :, n0:n0 + 512] = r[tl - SUBLANE:tl]
    for n0 in range(0, GDN_VAL, 512):
        z_ref[0, :, n0:n0 + 512] = _dot(h, wz_ref[:, n0:n0 + 512]).astype(z_ref.dtype)
    ab_ref[0] = _dot(h, wab_ref[...])


def _gdn_in(x, g, wqkv, wz, wab, act_dtype):
    B, L, D = x.shape
    tl = _row_tile(L, 700, BF16_ROWS)
    row = lambda n: pl.BlockSpec((1, tl, n), lambda b, l: (b, l, 0))
    return pl.pallas_call(
        functools.partial(_gdn_in_kernel, tl=tl),
        grid=(B, L // tl),
        in_specs=[row(D), _full((1, D)), _full((D, GDN_QKV)), _full((D, GDN_VAL)), _full((D, LANE))],
        out_specs=[row(GDN_QKV), row(GDN_VAL), row(LANE), pl.BlockSpec((1, SUBLANE, GDN_QKV), lambda b, l: (b, 0, 0))],
        out_shape=[jax.ShapeDtypeStruct((B, L, GDN_QKV), act_dtype), jax.ShapeDtypeStruct((B, L, GDN_VAL), act_dtype),
                   jax.ShapeDtypeStruct((B, L, LANE), F32), jax.ShapeDtypeStruct((B, SUBLANE, GDN_QKV), F32)],
        compiler_params=_cparams(("parallel", "arbitrary")),
        name="gdn_in",
    )(x, g, wqkv, wz, wab)


def _head_scalar(ref, sel):
    return jnp.sum(jnp.where(sel, ref[...], 0.0), axis=1, keepdims=True)


def _gdn_prepare_chunk(r0, C, first, hp, refs, consts):
    (q_ref, k_ref, v_ref, wq_ref, wk_ref, wv_ref, pq_ref, pk_ref, pv_ref, ab_ref, win_sc) = refs
    sel_a, sel_b, neg_ea, dtb = consts[hp]
    lanes = slice(hp * LANE, (hp + 1) * LANE)
    halo = BF16_ROWS

    def window(src_ref, prev_ref, slot):
        if first:
            win_sc[slot, 0:halo, :] = jnp.zeros((halo, LANE), F32)
            win_sc[slot, halo - (GDN_CONV - 1):halo, :] = prev_ref[0, :, lanes]
            win_sc[slot, halo:halo + C, :] = src_ref[0, 0:C, lanes].astype(F32)
            return win_sc[slot, 0:halo + C, :]
        return src_ref[0, pl.ds(r0 - halo, C + halo), lanes].astype(F32)

    def conv(src_ref, prev_ref, w_ref, slot):
        win = window(src_ref, prev_ref, slot)
        w = w_ref[:, lanes]
        acc = win[halo:halo + C] * w[GDN_CONV - 1:GDN_CONV]
        for j in range(GDN_CONV - 1):
            off = halo - (GDN_CONV - 1) + j
            acc = acc + win[off:off + C] * w[j:j + 1]
        return _silu(acc)

    def l2n(t):
        return t * lax.rsqrt(jnp.sum(t * t, axis=-1, keepdims=True) + NORM_EPS)

    q = l2n(conv(q_ref, pq_ref, wq_ref, 3 * hp)) * (GDN_DK ** -0.5)
    k = l2n(conv(k_ref, pk_ref, wk_ref, 3 * hp + 1))
    v = conv(v_ref, pv_ref, wv_ref, 3 * hp + 2)

    ab = ab_ref[0, pl.ds(r0, C), :]
    a_col = jnp.sum(jnp.where(sel_a, ab, 0.0), axis=1, keepdims=True)
    b_col = jnp.sum(jnp.where(sel_b, ab, 0.0), axis=1, keepdims=True)
    g_col = neg_ea * _softplus(a_col + dtb)
    beta = _sigmoid(b_col)

    ri = lax.broadcasted_iota(jnp.int32, (C, C), 0)
    ci = lax.broadcasted_iota(jnp.int32, (C, C), 1)
    eye = ri == ci
    causal = ri >= ci
    g_row = jnp.sum(jnp.where(eye, jnp.broadcast_to(g_col, (C, C)), 0.0), axis=0, keepdims=True)
    gc_col = jnp.sum(jnp.where(causal, jnp.broadcast_to(g_row, (C, C)), 0.0), axis=1, keepdims=True)
    gc_row = jnp.sum(jnp.where(eye, jnp.broadcast_to(gc_col, (C, C)), 0.0), axis=0, keepdims=True)
    decay = jnp.where(causal, jnp.exp(jnp.where(causal, gc_col - gc_row, 0.0)), 0.0)
    g_last = gc_row[:, C - 1:C]
    eg = jnp.exp(gc_col)

    kb = k * beta
    yield
    kqk = _dot_nt(jnp.concatenate([kb, q], axis=0).astype(BF16), k.astype(BF16))
    k_tail_t = jnp.transpose(k * jnp.exp(g_last - gc_col)).astype(BF16)
    yield
    A = jnp.where(ri > ci, kqk[:C] * decay, 0.0)
    attn = kqk[C:] * decay
    X = jnp.concatenate([v * beta, kb * eg], axis=-1)
    P = -A
    span = 1
    level = 0
    while span < C:
        split = level < GDN_SOLVE_SPLIT
        p_parts = _split_bf16(P) if split else (P.astype(BF16),)
        x_parts = _split_bf16(X) if split else (X.astype(BF16),)
        PX = _dot_parts(p_parts, x_parts)
        span *= 2
        level += 1
        if span < C:
            PP = _dot_parts(p_parts, p_parts)
        yield
        X = X + PX
        if span < C:
            P = PP
    u = X[:, :GDN_DV]
    w = X[:, GDN_DV:]
    wq = jnp.concatenate([w, q * eg], axis=0).astype(BF16)
    egl = jnp.broadcast_to(jnp.exp(g_last), (SUBLANE, LANE))
    return u, wq, k_tail_t, attn.astype(BF16), egl


def _gdn_apply_chunk(r0, C, hp, vals, s_sc, z_ref, onorm_ref, o_ref):
    u, wq, k_tail_t, attn, egl = vals
    lanes = slice(hp * LANE, (hp + 1) * LANE)
    S = s_sc[hp]
    wqS = _dot(wq, S.astype(BF16))
    yield
    v_new = u - wqS[:C]
    v_new_b = v_new.astype(BF16)
    kv = _dot(k_tail_t, v_new_b)
    av = _dot(attn, v_new_b)
    yield
    s_sc[hp] = S * egl[0:1, 0:1] + kv
    o = wqS[C:] + av
    o = o * lax.rsqrt(jnp.mean(o * o, axis=-1, keepdims=True) + NORM_EPS) * onorm_ref[...]
    o = o * _silu(z_ref[0, pl.ds(r0, C), lanes].astype(F32))
    o_ref[0, pl.ds(r0, C), lanes] = o.astype(o_ref.dtype)


def _gdn_core_kernel(q_ref, k_ref, v_ref, wq_ref, wk_ref, wv_ref, pq_ref, pk_ref, pv_ref, ab_ref, z_ref,
                     alog_ref, dtb_ref, onorm_ref, s0_ref, o_ref, sout_ref,
                     win_sc, s_sc, u_sc, wq_sc, kt_sc, at_sc, egl_sc, *, n_chunks, n_hp):
    C = GDN_CHUNK
    lane = lax.broadcasted_iota(jnp.int32, (1, LANE), 1)
    consts = []
    for hp in range(n_hp):
        head = pl.program_id(1) * n_hp + hp
        sel_a = lane == head
        consts.append((sel_a, lane == head + GDN_HEADS, -jnp.exp(_head_scalar(alog_ref, sel_a)),
                       _head_scalar(dtb_ref, sel_a)))
    refs = (q_ref, k_ref, v_ref, wq_ref, wk_ref, wv_ref, pq_ref, pk_ref, pv_ref, ab_ref, win_sc)
    tail = (s_sc, z_ref, onorm_ref, o_ref)

    def save(hp, vals):
        u_sc[hp], wq_sc[hp], kt_sc[hp], at_sc[hp], egl_sc[hp] = vals

    def load(hp):
        return u_sc[hp], wq_sc[hp], kt_sc[hp], at_sc[hp], egl_sc[hp]

    def row(c):
        r0 = N_META + c * C
        return r0 if isinstance(c, int) else pl.multiple_of(r0, BF16_ROWS)

    heads = range(n_hp)

    def prepare(r0, c_len, first):
        return [_gdn_prepare_chunk(r0, c_len, first, hp, refs, consts) for hp in heads]

    def apply(r0, c_len, vals):
        return [_gdn_apply_chunk(r0, c_len, hp, vals[hp], *tail) for hp in heads]

    s_sc[...] = s0_ref[0]
    _interleave(apply(0, N_META, _interleave(prepare(0, N_META, True))))
    for hp, vals in enumerate(_interleave(prepare(N_META, C, False))):
        save(hp, vals)

    def body(c, carry):
        vals = [load(hp) for hp in heads]
        out = _interleave(apply(row(c), C, vals) + prepare(row(c + 1), C, False))
        for hp in heads:
            save(hp, out[n_hp + hp])
        return carry

    lax.fori_loop(0, n_chunks - 1, body, 0)
    _interleave(apply(row(n_chunks - 1), C, [load(hp) for hp in heads]))
    sout_ref[0] = s_sc[...]


def _gdn_core(qkv, conv_w, conv_prev, ab, z, a_log, dt_bias, out_norm, S0):
    B, L, _ = qkv.shape
    n_chunks = (L - N_META) // GDN_CHUNK
    assert n_chunks >= 1 and N_META + n_chunks * GDN_CHUNK == L
    n_hp = 4
    ng = GDN_HEADS // n_hp
    wd = n_hp * LANE
    C = GDN_CHUNK
    col = lambda off: pl.BlockSpec((1, L, wd), lambda b, h: (b, 0, off + h))
    wcol = lambda off: pl.BlockSpec((GDN_CONV, wd), lambda b, h: (0, off + h))
    pcol = lambda off: pl.BlockSpec((1, GDN_CONV - 1, wd), lambda b, h: (b, 0, off + h))
    state = pl.BlockSpec((1, n_hp, GDN_DK, GDN_DV), lambda b, h: (b, h, 0, 0))
    return pl.pallas_call(
        functools.partial(_gdn_core_kernel, n_chunks=n_chunks, n_hp=n_hp),
        grid=(B, ng),
        in_specs=[col(0), col(ng), col(2 * ng), wcol(0), wcol(ng), wcol(2 * ng), pcol(0), pcol(ng), pcol(2 * ng),
                  pl.BlockSpec((1, L, LANE), lambda b, h: (b, 0, 0)), col(0),
                  _full((1, LANE)), _full((1, LANE)), _full((1, GDN_DV)), state],
        out_specs=[col(0), state],
        out_shape=[jax.ShapeDtypeStruct((B, L, GDN_VAL), BF16),
                   jax.ShapeDtypeStruct((B, GDN_HEADS, GDN_DK, GDN_DV), F32)],
        scratch_shapes=[pltpu.VMEM((3 * n_hp, BF16_ROWS + N_META, LANE), F32),
                        pltpu.VMEM((n_hp, GDN_DK, GDN_DV), F32),
                        pltpu.VMEM((n_hp, C, GDN_DV), F32), pltpu.VMEM((n_hp, 2 * C, GDN_DK), BF16),
                        pltpu.VMEM((n_hp, GDN_DK, C), BF16), pltpu.VMEM((n_hp, C, C), BF16),
                        pltpu.VMEM((n_hp, SUBLANE, LANE), F32)],
        compiler_params=_cparams(("parallel", "parallel")),
        name="gdn_core",
    )(qkv, qkv, qkv, conv_w, conv_w, conv_w, conv_prev, conv_prev, conv_prev, ab, z, a_log, dt_bias, out_norm, S0)


def _gdn_step_kernel(new_ref, prev_ref, w_ref, ab_ref, z_ref, alog_ref, dtb_ref, onorm_ref, s_ref, o_ref, sout_ref):
    w = w_ref[...]
    prev = prev_ref[0]
    y = new_ref[0] * w[GDN_CONV - 1:GDN_CONV]
    for j in range(GDN_CONV - 1):
        y = y + prev[j:j + 1] * w[j:j + 1]
    y = _silu(y)
    ab = ab_ref[0]
    z = z_ref[0]
    alog = alog_ref[...]
    dtb = dtb_ref[...]
    ri = lax.broadcasted_iota(jnp.int32, (GDN_DK, GDN_DK), 0)
    ci = lax.broadcasted_iota(jnp.int32, (GDN_DK, GDN_DK), 1)
    eye = ri == ci
    row8 = lax.broadcasted_iota(jnp.int32, (SUBLANE, GDN_DK), 0)

    def l2n(t):
        return t * lax.rsqrt(jnp.sum(t * t, axis=-1, keepdims=True) + NORM_EPS)

    for h in range(GDN_HEADS):
        q = l2n(y[:, h * GDN_DK:(h + 1) * GDN_DK]) * (GDN_DK ** -0.5)
        k = l2n(y[:, GDN_KEY + h * GDN_DK:GDN_KEY + (h + 1) * GDN_DK])
        v = y[:, 2 * GDN_KEY + h * GDN_DV:2 * GDN_KEY + (h + 1) * GDN_DV]
        g = -jnp.exp(alog[:, h:h + 1]) * _softplus(ab[:, h:h + 1] + dtb[:, h:h + 1])
        beta = _sigmoid(ab[:, GDN_HEADS + h:GDN_HEADS + h + 1])
        eg = jnp.exp(g)
        S = s_ref[0, h]
        kq = jnp.where(row8 == 0, jnp.broadcast_to(k, (SUBLANE, GDN_DK)),
                       jnp.where(row8 == 1, jnp.broadcast_to(q, (SUBLANE, GDN_DK)), 0.0))
        kqS = _dot_hi(kq, S)
        v_new = beta * (v - eg * kqS[0:1])
        qk = jnp.sum(q * k, axis=-1, keepdims=True)
        o = eg * kqS[1:2] + qk * v_new
        k_col = jnp.sum(jnp.where(eye, jnp.broadcast_to(k, (GDN_DK, GDN_DK)), 0.0), axis=1, keepdims=True)
        sout_ref[0, h] = S * eg + k_col * v_new
        o = o * lax.rsqrt(jnp.mean(o * o, axis=-1, keepdims=True) + NORM_EPS) * onorm_ref[...]
        o = o * _silu(z[:, h * GDN_DV:(h + 1) * GDN_DV])
        o_ref[0, :, h * GDN_DV:(h + 1) * GDN_DV] = o.astype(o_ref.dtype)


def _gdn_step(qkv_new, conv_prev, conv_w, ab, z, a_log, dt_bias, out_norm, S0):
    Bs = qkv_new.shape[0]
    per = lambda *s: pl.BlockSpec((1,) + s, lambda b: (b,) + (0,) * len(s))
    return pl.pallas_call(
        _gdn_step_kernel,
        grid=(Bs,),
        in_specs=[per(1, GDN_QKV), per(GDN_CONV - 1, GDN_QKV), _full((GDN_CONV, GDN_QKV)), per(1, LANE),
                  per(1, GDN_VAL), _full((1, LANE)), _full((1, LANE)), _full((1, GDN_DV)),
                  per(GDN_HEADS, GDN_DK, GDN_DV)],
        out_specs=[per(1, GDN_VAL), per(GDN_HEADS, GDN_DK, GDN_DV)],
        out_shape=[jax.ShapeDtypeStruct((Bs, 1, GDN_VAL), BF16),
                   jax.ShapeDtypeStruct((Bs, GDN_HEADS, GDN_DK, GDN_DV), F32)],
        compiler_params=_cparams(("parallel",)),
        name="gdn_step",
    )(qkv_new, conv_prev, conv_w, ab, z, a_log, dt_bias, out_norm, S0)


def _proj_residual_kernel(x_ref, o_ref, w_ref, g_ref, out_ref):
    y = _dot(o_ref[0], w_ref[...])
    out_ref[0] = x_ref[0] + _rms(y, g_ref[...])


def _proj_residual(x, o, w, g):
    B, L, D = x.shape
    K = o.shape[-1]
    tl = _row_tile(L, 700, BF16_ROWS)
    return pl.pallas_call(
        _proj_residual_kernel,
        grid=(B, L // tl),
        in_specs=[pl.BlockSpec((1, tl, D), lambda b, l: (b, l, 0)), pl.BlockSpec((1, tl, K), lambda b, l: (b, l, 0)),
                  _full((K, D)), _full((1, D))],
        out_specs=pl.BlockSpec((1, tl, D), lambda b, l: (b, l, 0)),
        out_shape=jax.ShapeDtypeStruct((B, L, D), F32),
        compiler_params=_cparams(("parallel", "parallel")),
        name="proj_residual",
    )(x, o, w, g)


def _ffn_act(j, u, tap1, tap0, cw_ref, cb_ref):
    cw = cw_ref[j]
    conv = u * cw[2:3] + tap1 * cw[1:2] + tap0 * cw[0:1] + cb_ref[j]
    return (_silu(conv[:, :FFN_CHUNK]) * conv[:, FFN_CHUNK:]).astype(BF16)


def _ffn_seq_kernel(x_ref, o_ref, wo_ref, go_ref, gpre_ref, wup_ref, cw_ref, cb_ref, wdn_ref, gpost_ref, prev_ref,
                    out_ref, newprev_ref, h_sc, acc_sc, a_sc, u_sc, carry_sc, *, tl):
    x = x_ref[0] + _rms(_dot(o_ref[0], wo_ref[...]), go_ref[...])
    out_ref[0] = x
    h_sc[...] = _rms(x, gpre_ref[...]).astype(BF16)
    acc_sc[...] = jnp.zeros_like(acc_sc)

    @pl.when(pl.program_id(1) == 0)
    def _():
        carry_sc[...] = jnp.zeros_like(carry_sc)
        carry_sc[:, SUBLANE - (FFN_CONV - 1):SUBLANE, :] = prev_ref[0]

    t_a = -(-(tl // 2) // BF16_ROWS) * BF16_ROWS

    def rows_chunk(r0, nr, j):
        u = _dot(h_sc[r0:r0 + nr, :], wup_ref[j])
        yield
        if r0 == 0:
            u_sc[0:SUBLANE, :] = carry_sc[j]
        u_sc[SUBLANE + r0:SUBLANE + r0 + nr, :] = u
        if r0 + nr == tl:
            carry_sc[j] = u[nr - SUBLANE:nr]
            newprev_ref[0, j] = u[nr - (FFN_CONV - 1):nr]
        lo = SUBLANE + r0
        a_sc[r0:r0 + nr, :] = _ffn_act(j, u, u_sc[lo - 1:lo - 1 + nr, :], u_sc[lo - 2:lo - 2 + nr, :], cw_ref, cb_ref)
        y = _dot(a_sc[r0:r0 + nr, :], wdn_ref[j])
        yield
        acc_sc[r0:r0 + nr, :] += y

    def body(j, carry):
        blocks = [rows_chunk(0, t_a, j)]
        if t_a < tl:
            blocks.append(rows_chunk(t_a, tl - t_a, j))
        _interleave(blocks)
        return carry

    lax.fori_loop(0, FFN_NCH, body, 0)
    out_ref[0] += _rms(acc_sc[...], gpost_ref[...])


def _ffn_step_kernel(x_ref, gpre_ref, wup_ref, cw_ref, cb_ref, wdn_ref, gpost_ref, prev_ref, out_ref, u_ref,
                     h_sc, acc_sc):
    x = x_ref[0]
    h_sc[...] = _rms(x, gpre_ref[...]).astype(BF16)
    acc_sc[...] = jnp.zeros_like(acc_sc)

    def body(j, carry):
        u = _dot(h_sc[...], wup_ref[j])
        u_ref[j] = u
        acc_sc[...] += _dot(_ffn_act(j, u, prev_ref[j, 1], prev_ref[j, 0], cw_ref, cb_ref), wdn_ref[j])
        return carry

    lax.fori_loop(0, FFN_NCH, body, 0)
    out_ref[0] = x + _rms(acc_sc[...], gpost_ref[...])


def _ffn_weight_specs():
    w2 = 2 * FFN_CHUNK
    once = pl.Buffered(1)
    return [_full((1, D_MODEL)), _full((FFN_NCH, D_MODEL, w2), once), _full((FFN_NCH, FFN_CONV, w2)),
            _full((FFN_NCH, 1, w2)), _full((FFN_NCH, FFN_CHUNK, D_MODEL), once), _full((1, D_MODEL))]


def _chunk_cols(t):
    lead = t.shape[:-1]
    t = t.reshape(lead + (2, FFN_NCH, FFN_CHUNK))
    t = jnp.moveaxis(t, -2, 0)
    return t.reshape((FFN_NCH,) + lead + (2 * FFN_CHUNK,))


def _unchunk_cols(t):
    lead = t.shape[1:-1]
    t = t.reshape((FFN_NCH,) + lead + (2, FFN_CHUNK))
    t = jnp.moveaxis(t, 0, -2)
    return t.reshape(lead + (2 * D_FF,))


def _ffn_seq(x, o, w_o, g_o, prev, wts):
    B, L, D = x.shape
    K = o.shape[-1]
    tl = _row_tile(L, 700, BF16_ROWS)
    w2 = 2 * FFN_CHUNK
    prev_c = jnp.moveaxis(_chunk_cols(prev), 0, 1)
    out, newprev = pl.pallas_call(
        functools.partial(_ffn_seq_kernel, tl=tl),
        grid=(B, L // tl),
        in_specs=[pl.BlockSpec((1, tl, D), lambda b, l: (b, l, 0)), pl.BlockSpec((1, tl, K), lambda b, l: (b, l, 0)),
                  _full((K, D), pl.Buffered(1)), _full((1, D))] + _ffn_weight_specs()
        + [pl.BlockSpec((1, FFN_NCH, FFN_CONV - 1, w2), lambda b, l: (b, 0, 0, 0))],
        out_specs=[pl.BlockSpec((1, tl, D), lambda b, l: (b, l, 0)),
                   pl.BlockSpec((1, FFN_NCH, FFN_CONV - 1, w2), lambda b, l: (b, 0, 0, 0))],
        out_shape=[jax.ShapeDtypeStruct((B, L, D), F32), jax.ShapeDtypeStruct((B, FFN_NCH, FFN_CONV - 1, w2), F32)],
        scratch_shapes=[pltpu.VMEM((tl, D), BF16), pltpu.VMEM((tl, D), F32), pltpu.VMEM((tl, FFN_CHUNK), BF16),
                        pltpu.VMEM((SUBLANE + tl, w2), F32),
                        pltpu.VMEM((FFN_NCH, SUBLANE, w2), F32)],
        compiler_params=_cparams(("parallel", "arbitrary")),
        name="ffn_seq",
    )(x, o, w_o, g_o, *wts, prev_c)
    return out, _unchunk_cols(jnp.moveaxis(newprev, 1, 0))


def _ffn_step(x, prev, wts):
    _, Bs, D = x.shape
    w2 = 2 * FFN_CHUNK
    prev_c = _chunk_cols(jnp.moveaxis(prev, 1, 0))
    out, u = pl.pallas_call(
        _ffn_step_kernel,
        grid=(1,),
        in_specs=[_full((1, Bs, D))] + _ffn_weight_specs() + [_full((FFN_NCH, FFN_CONV - 1, Bs, w2))],
        out_specs=[_full((1, Bs, D)), _full((FFN_NCH, Bs, w2))],
        out_shape=[jax.ShapeDtypeStruct((1, Bs, D), F32), jax.ShapeDtypeStruct((FFN_NCH, Bs, w2), F32)],
        scratch_shapes=[pltpu.VMEM((Bs, D), BF16), pltpu.VMEM((Bs, D), F32)],
        compiler_params=_cparams(("arbitrary",)),
        name="ffn_step",
    )(x, *wts, prev_c)
    return out, jnp.concatenate([prev[:, 1:], _unchunk_cols(u)[:, None]], axis=1)


def _rot_half(t, width):
    lane = lax.broadcasted_iota(jnp.int32, t.shape, t.ndim - 1)
    first = (lane % MLA_ROPE) < (MLA_ROPE // 2)
    return jnp.where(first, pltpu.roll(t, width - MLA_ROPE // 2, t.ndim - 1), pltpu.roll(t, MLA_ROPE // 2, t.ndim - 1))


def _mla_proj_kernel(x_ref, gkv_ref, wkva_ref, gkva_ref, gq_ref, wqa_ref, gqa_ref, wqb_ref, wuk_ref, cs_ref,
                     c_ref, kr_ref, kcat_ref, q_ref):
    x = x_ref[0]
    cos = cs_ref[:, 0:LANE]
    sin = cs_ref[:, LANE:2 * LANE]
    lane = lax.broadcasted_iota(jnp.int32, (1, LANE), 1)

    ckv = _dot(_rms(x, gkv_ref[...]).astype(BF16), wkva_ref[...])
    c = _rms(ckv[:, :MLA_KV_LORA], gkva_ref[...])
    r = ckv[:, MLA_KV_LORA:MLA_QK]
    kr = r * cos + _rot_half(r, LANE) * sin
    c_ref[0] = c
    kr_ref[0] = kr[:, :MLA_ROPE]
    kcat_ref[0, :, 0:MLA_KV_LORA] = c.astype(BF16)
    kcat_ref[0, :, MLA_KV_LORA:MLA_QK] = kr.astype(BF16)

    qa = _dot(_rms(x, gq_ref[...]).astype(BF16), wqa_ref[...])
    qb = _dot(_rms(qa, gqa_ref[...]).astype(BF16), wqb_ref[...])
    n_nope = MLA_HEADS * MLA_NOPE
    n_pe = MLA_HEADS * MLA_ROPE
    pe = qb[:, n_nope:n_nope + n_pe]
    reps = n_pe // LANE
    pe = pe * jnp.concatenate([cos] * reps, axis=1) + _rot_half(pe, n_pe) * jnp.concatenate([sin] * reps, axis=1)
    for h in range(MLA_HEADS):
        q_lat = _dot(qb[:, h * MLA_NOPE:(h + 1) * MLA_NOPE].astype(BF16), wuk_ref[h])
        q_ref[0, h, :, 0:MLA_KV_LORA] = (q_lat * MLA_SCALE).astype(BF16)
        t = pe[:, (h // 2) * LANE:(h // 2 + 1) * LANE]
        if h % 2 == 1:
            t = pltpu.roll(t, MLA_ROPE, 1)
        q_ref[0, h, :, MLA_KV_LORA:MLA_QK] = jnp.where(lane < MLA_ROPE, t * MLA_SCALE, 0.0).astype(BF16)


def _mla_proj(x, cs, gkv, wkva, gkva, gq, wqa, gqa, wqb, wuk):
    B, L, D = x.shape
    tl = _row_tile(L, 700, BF16_ROWS)
    row = lambda n: pl.BlockSpec((1, tl, n), lambda b, l: (b, l, 0))
    return pl.pallas_call(
        _mla_proj_kernel,
        grid=(B, L // tl),
        in_specs=[row(D), _full((1, D)), _full((D, MLA_QK)), _full((1, MLA_KV_LORA)), _full((1, D)),
                  _full((D, MLA_Q_LORA)), _full((1, MLA_Q_LORA)), _full(wqb.shape), _full(wuk.shape),
                  pl.BlockSpec((tl, 2 * LANE), lambda b, l: (l, 0))],
        out_specs=[row(MLA_KV_LORA), row(MLA_ROPE), row(MLA_QK),
                   pl.BlockSpec((1, MLA_HEADS, tl, MLA_QK), lambda b, l: (b, 0, l, 0))],
        out_shape=[jax.ShapeDtypeStruct((B, L, MLA_KV_LORA), F32), jax.ShapeDtypeStruct((B, L, MLA_ROPE), F32),
                   jax.ShapeDtypeStruct((B, L, MLA_QK), BF16),
                   jax.ShapeDtypeStruct((B, MLA_HEADS, L, MLA_QK), BF16)],
        compiler_params=_cparams(("parallel", "parallel")),
        name="mla_proj",
    )(x, gkv, wkva, gkva, gq, wqa, gqa, wqb, wuk, cs)


def _attn_kernel(q_ref, kcat_ref, wuv_ref, o_ref, m_sc, l_sc, acc_sc, *, tq, n_hp):
    i = pl.program_id(1)
    m_sc[...] = jnp.full_like(m_sc, -jnp.inf)
    l_sc[...] = jnp.zeros_like(l_sc)
    acc_sc[...] = jnp.zeros_like(acc_sc)
    rb = -(-tq // (ATTN_ROW_BLOCKS * BF16_ROWS)) * BF16_ROWS
    row_blocks = [(r0, min(rb, tq - r0)) for r0 in range(0, tq, rb)]

    def block(hp, r0, nr, kt, nk, masked):
        rows = slice(r0, r0 + nr)
        kv = kcat_ref[0, pl.ds(pl.multiple_of(kt * tq, BF16_ROWS), nk), :]
        s = _dot_nt(q_ref[0, hp, rows, :], kv)
        yield
        if masked:
            ri = lax.broadcasted_iota(jnp.int32, (nr, nk), 0)
            ci = lax.broadcasted_iota(jnp.int32, (nr, nk), 1)
            s = jnp.where(ci <= ri + r0, s, -jnp.inf)
        m_prev = m_sc[hp, rows]
        m_new = jnp.maximum(m_prev, jnp.max(s, axis=1, keepdims=True))
        alpha = jnp.exp(m_prev - m_new)
        p = jnp.exp(s - m_new)
        l_sc[hp, rows] = alpha * l_sc[hp, rows] + jnp.sum(p, axis=1, keepdims=True)
        m_sc[hp, rows] = m_new
        pv = _dot(p.astype(BF16), kv[:, :MLA_KV_LORA])
        yield
        acc_sc[hp, rows] = alpha * acc_sc[hp, rows] + pv

    def body(kt, carry):
        _interleave([block(hp, r0, nr, kt, tq, False) for hp in range(n_hp) for r0, nr in row_blocks])
        return carry

    lax.fori_loop(0, i, body, 0)
    _interleave([block(hp, r0, nr, i, r0 + nr, True) for hp in range(n_hp) for r0, nr in row_blocks])
    for hp in range(n_hp):
        o_lat = (acc_sc[hp] / l_sc[hp]).astype(BF16)
        o_ref[0, :, hp * MLA_V:(hp + 1) * MLA_V] = _dot(o_lat, wuv_ref[hp]).astype(o_ref.dtype)


def _attn_seq(q, kcat, wuv):
    B, nh, L, _ = q.shape
    tq = _row_tile(L, 700, BF16_ROWS)
    n_hp = 2
    return pl.pallas_call(
        functools.partial(_attn_kernel, tq=tq, n_hp=n_hp),
        grid=(B, L // tq, nh // n_hp),
        in_specs=[pl.BlockSpec((1, n_hp, tq, MLA_QK), lambda b, i, h: (b, h, i, 0)),
                  pl.BlockSpec((1, L, MLA_QK), lambda b, i, h: (b, 0, 0)),
                  pl.BlockSpec((n_hp, MLA_KV_LORA, MLA_V), lambda b, i, h: (h, 0, 0))],
        out_specs=pl.BlockSpec((1, tq, n_hp * MLA_V), lambda b, i, h: (b, i, h)),
        out_shape=jax.ShapeDtypeStruct((B, L, nh * MLA_V), BF16),
        scratch_shapes=[pltpu.VMEM((n_hp, tq, 1), F32), pltpu.VMEM((n_hp, tq, 1), F32),
                        pltpu.VMEM((n_hp, tq, MLA_KV_LORA), F32)],
        compiler_params=_cparams(("parallel", "parallel", "parallel")),
        name="attn_seq",
    )(q, kcat, wuv)


def _attn_paged_kernel(pt_ref, q_ref, kcat_ref, cache_c_ref, cache_k_ref, o_ref, cbuf, kbuf, c16, s_sc, sem,
                       *, n_pages, page, kchunk):
    b = pl.program_id(0)
    nb = pl.num_programs(0)
    P = n_pages * page

    def copies(bb, slot):
        out = []
        for pg in range(n_pages):
            idx = pt_ref[bb, pg]
            out.append(pltpu.make_async_copy(cache_c_ref.at[idx], cbuf.at[slot, pl.ds(pg * page, page), :],
                                             sem.at[0, slot]))
            out.append(pltpu.make_async_copy(cache_k_ref.at[idx], kbuf.at[slot, :, pl.ds(pg * page, page)],
                                             sem.at[1, slot]))
        return out

    @pl.when(b == 0)
    def _():
        for cp in copies(0, 0):
            cp.start()

    @pl.when(b + 1 < nb)
    def _():
        for cp in copies(b + 1, (b + 1) % 2):
            cp.start()

    slot = b % 2
    for cp in copies(b, slot):
        cp.wait()

    q = q_ref[0]
    q_lat = q[:, :MLA_KV_LORA]
    q_pe = q[:, MLA_KV_LORA:MLA_KV_LORA + MLA_ROPE]
    for c0 in range(0, P, kchunk):
        cc = cbuf[slot, c0:c0 + kchunk, :].astype(BF16)
        c16[c0:c0 + kchunk, :] = cc
        kk = kbuf[slot, :, c0:c0 + kchunk].astype(BF16)
        s_sc[:, c0:c0 + kchunk] = _dot_nt(q_lat, cc) + _dot(q_pe, kk)
    kself = kcat_ref[0]
    s_self = jnp.sum(q.astype(F32) * kself.astype(F32), axis=-1, keepdims=True)
    s = s_sc[...]
    m = jnp.maximum(jnp.max(s, axis=-1, keepdims=True), s_self)
    p = jnp.exp(s - m)
    p_self = jnp.exp(s_self - m)
    denom = jnp.sum(p, axis=-1, keepdims=True) + p_self
    s_sc[...] = p
    acc = p_self * kself[:, :MLA_KV_LORA].astype(F32)
    for c0 in range(0, P, kchunk):
        acc = acc + _dot(s_sc[:, c0:c0 + kchunk].astype(BF16), c16[c0:c0 + kchunk, :])
    o_ref[0] = (acc / denom).astype(o_ref.dtype)


def _attn_paged(q, kcat, cache_c, cache_k, page_table):
    Bs, nh, _ = q.shape
    n_pages = page_table.shape[1]
    page = cache_c.shape[1]
    cache_k = jnp.swapaxes(cache_k, 1, 2)
    P = n_pages * page
    kchunk = _row_tile(P, 1024, LANE)
    grid_spec = pltpu.PrefetchScalarGridSpec(
        num_scalar_prefetch=1,
        grid=(Bs,),
        in_specs=[pl.BlockSpec((1, nh, MLA_QK), lambda b, pt: (b, 0, 0)),
                  pl.BlockSpec((1, 1, MLA_QK), lambda b, pt: (b, 0, 0)),
                  pl.BlockSpec(memory_space=pl.ANY), pl.BlockSpec(memory_space=pl.ANY)],
        out_specs=pl.BlockSpec((1, nh, MLA_KV_LORA), lambda b, pt: (b, 0, 0)),
        scratch_shapes=[pltpu.VMEM((2, P, MLA_KV_LORA), F32), pltpu.VMEM((2, MLA_ROPE, P), F32),
                        pltpu.VMEM((P, MLA_KV_LORA), BF16), pltpu.VMEM((nh, P), F32),
                        pltpu.SemaphoreType.DMA((2, 2))],
    )
    return pl.pallas_call(
        functools.partial(_attn_paged_kernel, n_pages=n_pages, page=page, kchunk=kchunk),
        grid_spec=grid_spec,
        out_shape=jax.ShapeDtypeStruct((Bs, nh, MLA_KV_LORA), BF16),
        compiler_params=_cparams(("arbitrary",)),
        name="attn_paged",
    )(page_table, q, kcat, cache_c, cache_k)


def _mla_out_kernel(x_ref, o_ref, wuv_ref, wout_ref, g_ref, out_ref, acc_sc):
    for h in range(MLA_HEADS):
        oh = _dot(o_ref[0, h], wuv_ref[h]).astype(BF16)
        y = _dot(oh, wout_ref[h])
        if h == 0:
            acc_sc[...] = y
        else:
            acc_sc[...] += y
    out_ref[0] = x_ref[0] + _rms(acc_sc[...], g_ref[...])


def _mla_out(x, o_lat, wuv, wout, g):
    B, L, D = x.shape
    tl = _row_tile(L, 700, BF16_ROWS)
    return pl.pallas_call(
        _mla_out_kernel,
        grid=(B, L // tl),
        in_specs=[pl.BlockSpec((1, tl, D), lambda b, l: (b, l, 0)),
                  pl.BlockSpec((1, MLA_HEADS, tl, MLA_KV_LORA), lambda b, l: (b, 0, l, 0)),
                  _full(wuv.shape), _full(wout.shape), _full((1, D))],
        out_specs=pl.BlockSpec((1, tl, D), lambda b, l: (b, l, 0)),
        out_shape=jax.ShapeDtypeStruct((B, L, D), F32),
        scratch_shapes=[pltpu.VMEM((tl, D), F32)],
        compiler_params=_cparams(("parallel", "parallel")),
        name="mla_out",
    )(x, o_lat, wuv, wout, g)


def _rope_table(positions):
    half = MLA_ROPE // 2
    inv = ROPE_THETA ** (-jnp.arange(half, dtype=F32) / half)
    ang = positions.astype(F32)[:, None] * inv[None, :]
    cos, sin = jnp.cos(ang), jnp.sin(ang)
    reps = LANE // MLA_ROPE
    return jnp.concatenate([jnp.concatenate([cos, cos] * reps, axis=1), jnp.concatenate([-sin, sin] * reps, axis=1)],
                           axis=1)


def _row(v, width=None):
    v = v.reshape(1, -1).astype(F32)
    if width is not None and v.shape[1] < width:
        v = jnp.pad(v, ((0, 0), (0, width - v.shape[1])))
    return v


def _prepare(p):
    i = 0
    w_in = p['a_w_in'][i]
    n_ab = 2 * GDN_HEADS
    wq_b = p['b_w_q_b'][i].reshape(MLA_Q_LORA, MLA_HEADS, MLA_NOPE + MLA_ROPE)
    wq_b = jnp.concatenate([wq_b[:, :, :MLA_NOPE].reshape(MLA_Q_LORA, -1), wq_b[:, :, MLA_NOPE:].reshape(MLA_Q_LORA, -1)],
                           axis=1)
    prep = {
        'a_norm_pre': _row(p['a_norm_pre'][i]), 'a_norm_post': _row(p['a_norm_post'][i]),
        'a_wqkv': w_in[:, :GDN_QKV].astype(BF16), 'a_wz': w_in[:, GDN_QKV:GDN_QKV + GDN_VAL].astype(BF16),
        'a_wab': jnp.pad(w_in[:, GDN_QKV + GDN_VAL:], ((0, 0), (0, LANE - n_ab))).astype(BF16),
        'a_conv_w': p['a_conv_w'][i].astype(F32), 'a_log': _row(p['a_log'][i], LANE),
        'a_dt_bias': _row(p['a_dt_bias'][i], LANE), 'a_out_norm': _row(p['a_out_norm'][i]),
        'a_w_out': p['a_w_out'][i].astype(BF16),
        'kv_norm': _row(p['kv_norm']),
        'kv_w_a': jnp.pad(p['kv_w_a'], ((0, 0), (0, MLA_QK - MLA_KV_LORA - MLA_ROPE))).astype(BF16),
        'kv_a_norm': _row(p['kv_a_norm']),
        'kv_w_uk': jnp.transpose(p['kv_w_uk'], (1, 2, 0)).astype(BF16),
        'kv_w_uv': jnp.transpose(p['kv_w_uv'], (1, 0, 2)).astype(BF16),
        'b_norm_pre': _row(p['b_norm_pre'][i]), 'b_norm_post': _row(p['b_norm_post'][i]),
        'b_w_q_a': p['b_w_q_a'][i].astype(BF16), 'b_q_a_norm': _row(p['b_q_a_norm'][i]),
        'b_w_q_b': wq_b.astype(BF16),
        'b_w_out': p['b_w_out'][i].reshape(MLA_HEADS, MLA_V, D_MODEL).astype(BF16),
        'ffn': [],
    }
    for layer in range(2):
        w_up = jnp.moveaxis(_chunk_cols(p['f_w_up'][layer]), 0, 0).astype(BF16)
        prep['ffn'].append((
            _row(p['f_norm_pre'][layer]), w_up, _chunk_cols(p['f_conv_w'][layer]).astype(F32),
            _chunk_cols(p['f_conv_b'][layer].reshape(1, -1)).astype(F32),
            p['f_w_down'][layer].reshape(FFN_NCH, FFN_CHUNK, D_MODEL).astype(BF16), _row(p['f_norm_post'][layer])))
    return prep


def _mla_block(x, cs, prep):
    return _mla_proj(x, cs, prep['kv_norm'], prep['kv_w_a'], prep['kv_a_norm'], prep['b_norm_pre'], prep['b_w_q_a'],
                     prep['b_q_a_norm'], prep['b_w_q_b'], prep['kv_w_uk'])


def _trunk_seq(x, S0, dconv0, fconv0, prep):
    B, L, _ = x.shape
    assert L >= SUBLANE
    qkv, z, ab, tail = _gdn_in(x, prep['a_norm_pre'], prep['a_wqkv'], prep['a_wz'], prep['a_wab'], BF16)
    o, S = _gdn_core(qkv, prep['a_conv_w'], dconv0[0], ab, z, prep['a_log'], prep['a_dt_bias'], prep['a_out_norm'], S0[0])
    dconv = tail[:, SUBLANE - (GDN_CONV - 1):]
    x, fconv_a = _ffn_seq(x, o, prep['a_w_out'], prep['a_norm_post'], fconv0[0], prep['ffn'][0])
    cs = _rope_table(jnp.arange(L, dtype=jnp.int32))
    c, kr, kcat, q = _mla_block(x, cs, prep)
    o = _attn_seq(q, kcat, prep['kv_w_uv'])
    x, fconv_b = _ffn_seq(x, o, prep['b_w_out'].reshape(MLA_HEADS * MLA_V, D_MODEL), prep['b_norm_post'], fconv0[1],
                          prep['ffn'][1])
    return x, S[None], dconv[None], jnp.stack([fconv_a, fconv_b]), c, kr


def _trunk_step(x, pos, S0, dconv0, fconv0, cache_c, cache_k, page_table, prep):
    Bs = x.shape[0]
    xt = x.reshape(1, Bs, D_MODEL)
    qkv, z, ab, _ = _gdn_in(xt, prep['a_norm_pre'], prep['a_wqkv'], prep['a_wz'], prep['a_wab'], F32)
    o, S = _gdn_step(qkv.reshape(Bs, 1, -1), dconv0[0], prep['a_conv_w'], ab.reshape(Bs, 1, -1), z.reshape(Bs, 1, -1),
                     prep['a_log'], prep['a_dt_bias'], prep['a_out_norm'], S0[0])
    dconv = jnp.concatenate([dconv0[0][:, 1:], qkv.reshape(Bs, 1, -1)], axis=1)
    xt = _proj_residual(xt, o.reshape(1, Bs, -1), prep['a_w_out'], prep['a_norm_post'])
    xt, fconv_a = _ffn_step(xt, fconv0[0], prep['ffn'][0])
    cs = _rope_table(jnp.full((Bs,), pos, jnp.int32))
    c, kr, kcat, q = _mla_block(xt, cs, prep)
    q = jnp.moveaxis(q[0], 0, 1)
    o_lat = _attn_paged(q, kcat.reshape(Bs, 1, MLA_QK), cache_c, cache_k, page_table)
    o_lat = jnp.moveaxis(o_lat, 0, 1)[None]
    xt = _mla_out(xt, o_lat, prep['kv_w_uv'], prep['b_w_out'], prep['b_norm_post'])
    xt, fconv_b = _ffn_step(xt, fconv0[1], prep['ffn'][1])
    return (xt.reshape(Bs, 1, D_MODEL), S[None], dconv[None], jnp.stack([fconv_a, fconv_b]),
            c.reshape(Bs, 1, MLA_KV_LORA), kr.reshape(Bs, 1, MLA_ROPE))


def kernel(x_prompt, x_sample, state_delta_S, state_delta_conv, state_ffn_conv, cache_kv_latent, cache_k_rope,
           page_table, meta_tokens, a_norm_pre, a_norm_post, a_w_in, a_conv_w, a_log, a_dt_bias, a_out_norm, a_w_out,
           kv_norm, kv_w_a, kv_a_norm, kv_w_uk, kv_w_uv, b_norm_pre, b_norm_post, b_w_q_a, b_q_a_norm, b_w_q_b,
           b_w_out, f_norm_pre, f_norm_post, f_w_up, f_conv_w, f_conv_b, f_w_down):
    prep = _prepare({
        'a_norm_pre': a_norm_pre, 'a_norm_post': a_norm_post, 'a_w_in': a_w_in, 'a_conv_w': a_conv_w, 'a_log': a_log,
        'a_dt_bias': a_dt_bias, 'a_out_norm': a_out_norm, 'a_w_out': a_w_out, 'kv_norm': kv_norm, 'kv_w_a': kv_w_a,
        'kv_a_norm': kv_a_norm, 'kv_w_uk': kv_w_uk, 'kv_w_uv': kv_w_uv, 'b_norm_pre': b_norm_pre,
        'b_norm_post': b_norm_post, 'b_w_q_a': b_w_q_a, 'b_q_a_norm': b_q_a_norm, 'b_w_q_b': b_w_q_b,
        'b_w_out': b_w_out, 'f_norm_pre': f_norm_pre, 'f_norm_post': f_norm_post, 'f_w_up': f_w_up,
        'f_conv_w': f_conv_w, 'f_conv_b': f_conv_b, 'f_w_down': f_w_down})

    bp = x_prompt.shape[0]
    xp = jnp.concatenate([jnp.broadcast_to(meta_tokens.astype(x_prompt.dtype)[None], (bp, N_META, D_MODEL)), x_prompt],
                         axis=1)
    zS = jnp.zeros((1, bp, GDN_HEADS, GDN_DK, GDN_DV), state_delta_S.dtype)
    zdc = jnp.zeros((1, bp, GDN_CONV - 1, GDN_QKV), x_prompt.dtype)
    zfc = jnp.zeros((2, bp, FFN_CONV - 1, 2 * D_FF), x_prompt.dtype)
    yp, p_S, p_dconv, p_fconv, p_c, p_kr = _trunk_seq(xp, zS, zdc, zfc, prep)

    past_len = page_table.shape[1] * cache_kv_latent.shape[1]
    ys, s_S, s_dconv, s_fconv, s_c, s_kr = _trunk_step(x_sample, past_len, state_delta_S, state_delta_conv,
                                                       state_ffn_conv, cache_kv_latent, cache_k_rope, page_table, prep)
    return (yp[:, N_META:], ys, p_S, p_dconv, p_fconv, p_c, p_kr, s_S, s_dconv, s_fconv, s_c, s_kr)
```
